```python
import math
import jax, jax.numpy as jnp
from jax import lax
import numpy as np

D_MODEL = 1024
BATCH = 4
SEQ = 8192
DEPTH = 1

CHUNK = 64
A_HEADS = 8
A_DK = 128
A_DV = 128
A_CONV = 4
A_W = A_HEADS * A_DV
B_HEADS = 16
B_DH = 64
B_W = B_HEADS * B_DH
B_PREV_CHUNKS = 8
B_MAX_REL = 256
B_REL_SIZE = CHUNK - 1 + B_MAX_REL + 1
D_FF = 2816
FFN_CONV = 3
N_BRANCHES = 2
IN_SPLITS = (3 * A_W, 4 * A_W, 4 * A_W + A_HEADS, 4 * A_W + 2 * A_HEADS,
             4 * A_W + 2 * A_HEADS + B_W, 4 * A_W + 2 * A_HEADS + 2 * B_W,
             4 * A_W + 2 * A_HEADS + 3 * B_W)
IN_COLS = 4 * A_W + 2 * A_HEADS + 3 * B_W + N_BRANCHES * D_MODEL
DEEPNORM_ALPHA = (2.0 * DEPTH) ** 0.25
DEEPNORM_BETA = (8.0 * DEPTH) ** -0.25
LN_EPS = 1e-5
RMS_EPS = 1e-6
L2_EPS = 1e-6
NEG_INF = -1e30

kernel_name = "hybrid_deltanet_bandattn_convffn_deepnorm_adaln"


def layernorm(x, g, b):
    xf = x.astype(jnp.float32)
    mu = jnp.mean(xf, axis=-1, keepdims=True)
    var = jnp.mean(jnp.square(xf - mu), axis=-1, keepdims=True)
    y = (xf - mu) * lax.rsqrt(var + LN_EPS) * g.astype(jnp.float32) + b.astype(jnp.float32)
    return y.astype(x.dtype)


def causal_dwconv(x, w):
    k_width, ch = w.shape
    return lax.conv_general_dilated(
        x, w[:, None, :].astype(x.dtype), window_strides=(1,), padding=[(k_width - 1, 0)],
        dimension_numbers=("NWC", "WIO", "NWC"), feature_group_count=ch)


def l2norm(x):
    return x * lax.rsqrt(jnp.sum(jnp.square(x), axis=-1, keepdims=True) + L2_EPS)


def chunk_gated_delta_rule(q, k, v, g, beta):
    b_, s_, h_, dk = q.shape
    dv = v.shape[-1]
    n_chunks = s_ // CHUNK

    def to_chunks(t):
        return t.reshape(b_, n_chunks, CHUNK, h_, -1).transpose(0, 1, 3, 2, 4)

    q, k, v = to_chunks(q), to_chunks(k), to_chunks(v)
    g = g.reshape(b_, n_chunks, CHUNK, h_).transpose(0, 1, 3, 2)
    beta = beta.reshape(b_, n_chunks, CHUNK, h_).transpose(0, 1, 3, 2)
    g = jnp.cumsum(g, axis=-1)

    causal = jnp.tril(jnp.ones((CHUNK, CHUNK), dtype=bool))
    strict = jnp.tril(jnp.ones((CHUNK, CHUNK), dtype=bool), k=-1)
    diff = g[..., :, None] - g[..., None, :]
    decay = jnp.where(causal, jnp.exp(jnp.where(causal, diff, 0.0)), 0.0)

    k_beta = k * beta[..., None]
    v_beta = v * beta[..., None]
    a_low = jnp.where(strict, jnp.einsum("bnhid,bnhjd->bnhij", k_beta, k) * decay, 0.0)
    eye = jnp.eye(CHUNK, dtype=jnp.float32)
    rhs = jnp.concatenate([v_beta, k_beta * jnp.exp(g)[..., None]], axis=-1)
    sol = lax.linalg.triangular_solve(a_low + eye, rhs, left_side=True, lower=True,
                                      unit_diagonal=True)
    u = sol[..., :dv]
    w = sol[..., dv:]
    qk = jnp.where(causal, jnp.einsum("bnhid,bnhjd->bnhij", q, k) * decay, 0.0)

    def step(state, inp):
        q_n, k_n, u_n, w_n, qk_n, g_n = inp
        v_new = u_n - jnp.einsum("bhck,bhkv->bhcv", w_n, state)
        o_n = (jnp.einsum("bhck,bhkv->bhcv", q_n * jnp.exp(g_n)[..., None], state)
               + jnp.einsum("bhij,bhjv->bhiv", qk_n, v_new))
        g_last = g_n[..., -1]
        k_dec = k_n * jnp.exp(g_last[..., None] - g_n)[..., None]
        state = state * jnp.exp(g_last)[..., None, None] + jnp.einsum("bhck,bhcv->bhkv", k_dec, v_new)
        return state, o_n

    xs = tuple(jnp.moveaxis(t, 1, 0) for t in (q, k, u, w, qk, g))
    state0 = jnp.zeros((b_, h_, dk, dv), jnp.float32)
    _, o = lax.scan(step, state0, xs)
    return o.transpose(1, 0, 3, 2, 4).reshape(b_, s_, h_, dv)


def gated_deltanet(qkv, z, beta_raw, a_raw, conv_w, a_log, dt_bias, norm_w):
    b_, s_, _ = qkv.shape
    qkv = jax.nn.silu(causal_dwconv(qkv, conv_w))
    q, k, v = jnp.split(qkv.astype(jnp.float32), 3, axis=-1)
    q = l2norm(q.reshape(b_, s_, A_HEADS, A_DK)) * (A_DK ** -0.5)
    k = l2norm(k.reshape(b_, s_, A_HEADS, A_DK))
    v = v.reshape(b_, s_, A_HEADS, A_DV)
    beta = jax.nn.sigmoid(beta_raw.astype(jnp.float32))
    g = -jnp.exp(a_log.astype(jnp.float32)) * jax.nn.softplus(
        a_raw.astype(jnp.float32) + dt_bias.astype(jnp.float32))
    o = chunk_gated_delta_rule(q, k, v, g, beta)
    o = o * lax.rsqrt(jnp.mean(jnp.square(o), axis=-1, keepdims=True) + RMS_EPS)
    o = o * norm_w.astype(jnp.float32) * jax.nn.silu(z.astype(jnp.float32).reshape(b_, s_, A_HEADS, A_DV))
    return o.reshape(b_, s_, A_W).astype(qkv.dtype)


def chunk_band_attention(q, k, v, rel_bias):
    b_, s_, h_, dh = q.shape
    n_chunks = s_ // CHUNK
    pad = B_PREV_CHUNKS * CHUNK
    band = (B_PREV_CHUNKS + 1) * CHUNK
    k_pad = jnp.pad(k, ((0, 0), (pad, 0), (0, 0), (0, 0)))
    v_pad = jnp.pad(v, ((0, 0), (pad, 0), (0, 0), (0, 0)))
    qi = np.arange(CHUNK)[:, None]
    kj = np.arange(band)[None, :]
    dist = pad + qi - kj
    idx = np.clip(dist, -(CHUNK - 1), B_MAX_REL) + (CHUNK - 1)
    bias = rel_bias.astype(jnp.float32)[:, idx]
    key_chunk = jnp.arange(band) // CHUNK
    scale = dh ** -0.5

    def one_chunk(n):
        q_n = lax.dynamic_slice_in_dim(q, n * CHUNK, CHUNK, axis=1)
        k_n = lax.dynamic_slice_in_dim(k_pad, n * CHUNK, band, axis=1)
        v_n = lax.dynamic_slice_in_dim(v_pad, n * CHUNK, band, axis=1)
        valid = (n - B_PREV_CHUNKS + key_chunk) >= 0
        s = jnp.einsum("bqhd,bkhd->bhqk", q_n, k_n).astype(jnp.float32) * scale + bias
        s = jnp.where(valid, s, NEG_INF)
        p = jax.nn.softmax(s, axis=-1).astype(v.dtype)
        return jnp.einsum("bhqk,bkhd->bqhd", p, v_n)

    out = lax.map(one_chunk, jnp.arange(n_chunks))
    return out.transpose(1, 0, 2, 3, 4).reshape(b_, s_, h_ * dh)


def setup_inputs(seed: int = 0) -> dict:
    key = jax.random.key(seed)
    ks = jax.random.split(key, 24)

    def nrm(k, shape, scale):
        return jax.random.normal(k, shape, jnp.float32) * scale

    L = DEPTH
    x = nrm(ks[0], (BATCH, SEQ, D_MODEL), 1.0)
    c = nrm(ks[1], (BATCH, D_MODEL), 1.0)
    w_ada = nrm(ks[2], (L, D_MODEL, 6 * D_MODEL), D_MODEL ** -0.5)
    b_ada = nrm(ks[3], (L, 6 * D_MODEL), 0.02)
    w_in = nrm(ks[4], (L, D_MODEL, IN_COLS), D_MODEL ** -0.5)
    b_gate = nrm(ks[5], (L, N_BRANCHES * D_MODEL), 0.1)
    conv_a = nrm(ks[6], (L, A_CONV, 3 * A_W), A_CONV ** -0.5)
    a_log = jnp.log(jax.random.uniform(ks[7], (L, A_HEADS), jnp.float32, minval=1.0, maxval=16.0))
    dt = jnp.exp(jax.random.uniform(ks[8], (L, A_HEADS), jnp.float32,
                                    minval=math.log(1e-3), maxval=math.log(1e-1)))
    dt_bias = dt + jnp.log(-jnp.expm1(-dt))
    norm_a = 1.0 + nrm(ks[9], (L, A_DV), 0.05)
    rel_bias = nrm(ks[10], (L, B_HEADS, B_REL_SIZE), 0.2)
    w_branch_a = nrm(ks[11], (L, A_W, D_MODEL), A_W ** -0.5)
    w_branch_b = nrm(ks[12], (L, B_W, D_MODEL), B_W ** -0.5)
    w_o = nrm(ks[13], (L, D_MODEL, D_MODEL), DEEPNORM_BETA * D_MODEL ** -0.5)
    ln1_g = 1.0 + nrm(ks[14], (L, D_MODEL), 0.05)
    ln1_b = nrm(ks[15], (L, D_MODEL), 0.02)
    w_up = nrm(ks[16], (L, D_MODEL, 2 * D_FF), D_MODEL ** -0.5)
    conv_ffn = nrm(ks[17], (L, FFN_CONV, 2 * D_FF), FFN_CONV ** -0.5)
    b_conv_ffn = nrm(ks[18], (L, 2 * D_FF), 0.02)
    w_down = nrm(ks[19], (L, D_FF, D_MODEL), DEEPNORM_BETA * D_FF ** -0.5)
    ln2_g = 1.0 + nrm(ks[20], (L, D_MODEL), 0.05)
    ln2_b = nrm(ks[21], (L, D_MODEL), 0.02)
    return {"x": x, "c": c, "w_ada": w_ada, "b_ada": b_ada, "w_in": w_in, "b_gate": b_gate,
            "conv_a": conv_a, "a_log": a_log, "dt_bias": dt_bias, "norm_a": norm_a,
            "rel_bias": rel_bias, "w_branch_a": w_branch_a, "w_branch_b": w_branch_b, "w_o": w_o,
            "ln1_g": ln1_g, "ln1_b": ln1_b, "w_up": w_up, "conv_ffn": conv_ffn,
            "b_conv_ffn": b_conv_ffn, "w_down": w_down, "ln2_g": ln2_g, "ln2_b": ln2_b}


def reference(x, c, w_ada, b_ada, w_in, b_gate, conv_a, a_log, dt_bias, norm_a, rel_bias,
              w_branch_a, w_branch_b, w_o, ln1_g, ln1_b, w_up, conv_ffn, b_conv_ffn, w_down,
              ln2_g, ln2_b):
    b_, s_, d_ = x.shape
    c_act = jax.nn.silu(c)
    for l in range(DEPTH):
        mod = (c_act @ w_ada[l] + b_ada[l])[:, None, :]
        shift_t, scale_t, gate_t, shift_f, scale_f, gate_f = jnp.split(mod, 6, axis=-1)

        h = x * (1.0 + scale_t) + shift_t
        proj = h @ w_in[l]
        qkv_a, z_a, beta_raw, a_raw, q_b, k_b, v_b, gates = jnp.split(proj, IN_SPLITS, axis=-1)

        o_a = gated_deltanet(qkv_a, z_a, beta_raw, a_raw, conv_a[l], a_log[l], dt_bias[l], norm_a[l])
        o_b = chunk_band_attention(q_b.reshape(b_, s_, B_HEADS, B_DH),
                                   k_b.reshape(b_, s_, B_HEADS, B_DH),
                                   v_b.reshape(b_, s_, B_HEADS, B_DH), rel_bias[l])

        gate_a, gate_b = jnp.split(jax.nn.sigmoid(gates + b_gate[l]), N_BRANCHES, axis=-1)
        merged = gate_a * (o_a @ w_branch_a[l]) + gate_b * (o_b @ w_branch_b[l])
        mix = merged @ w_o[l]
        x = layernorm(DEEPNORM_ALPHA * x + gate_t * mix, ln1_g[l], ln1_b[l])

        h = x * (1.0 + scale_f) + shift_f
        u = causal_dwconv(h @ w_up[l], conv_ffn[l]) + b_conv_ffn[l]
        u_gate, u_val = jnp.split(u, 2, axis=-1)
        ffn = (jax.nn.silu(u_gate) * u_val) @ w_down[l]
        x = layernorm(DEEPNORM_ALPHA * x + gate_f * ffn, ln2_g[l], ln2_b[l])
    return x
```

```python
import functools
import math

import jax
import jax.numpy as jnp
import numpy as np
from jax import lax
from jax.experimental import pallas as pl
from jax.experimental.pallas import tpu as pltpu

F32 = jnp.float32
BF16 = jnp.bfloat16

CHUNK = 64
A_HEADS = 8
A_DK = 128
A_DV = 128
A_CONV = 4
A_W = A_HEADS * A_DV
B_HEADS = 16
B_DH = 64
B_W = B_HEADS * B_DH
B_PREV_CHUNKS = 8
B_MAX_REL = 256
BAND = (B_PREV_CHUNKS + 1) * CHUNK
FFN_CONV = 3
LN_EPS = 1e-5
RMS_EPS = 1e-6
L2_EPS = 1e-6
NEG_INF = -1e30

LANES = 128
SUBLANES = 8
VMEM_LIMIT = 56 * 1024 * 1024


def _mm(a, b):
    return jnp.dot(a.astype(BF16), b.astype(BF16), preferred_element_type=F32)


def _mm_nt(a, b):
    return lax.dot_general(a.astype(BF16), b.astype(BF16), (((1,), (1,)), ((), ())),
                           preferred_element_type=F32)


def _sigmoid(x):
    return 1.0 / (1.0 + jnp.exp(-x))


def _silu(x):
    return x * _sigmoid(x)


def _softplus(x):
    return jnp.maximum(x, 0.0) + jnp.log(1.0 + jnp.exp(-jnp.abs(x)))


def _layernorm(y, g, b):
    mu = jnp.mean(y, axis=-1, keepdims=True)
    d = y - mu
    var = jnp.mean(d * d, axis=-1, keepdims=True)
    return d * lax.rsqrt(var + LN_EPS) * g + b


def _const_spec(shape):
    return pl.BlockSpec(shape, lambda *_: (0,) * len(shape), pipeline_mode=pl.Buffered(1))


def _ada_kernel(c_ref, w_ref, b_ref, o_ref):
    c = c_ref[...]
    o_ref[...] = jnp.dot(_silu(c), w_ref[...], preferred_element_type=F32,
                         precision=lax.Precision.HIGHEST) + b_ref[...]


def _ada(c_pad, w, b):
    rows, d = c_pad.shape
    n = w.shape[1]
    tn = 1536
    return pl.pallas_call(
        _ada_kernel,
        grid=(n // tn,),
        in_specs=[pl.BlockSpec((rows, d), lambda j: (0, 0)),
                  pl.BlockSpec((d, tn), lambda j: (0, j)),
                  pl.BlockSpec((1, tn), lambda j: (0, j))],
        out_specs=pl.BlockSpec((rows, tn), lambda j: (0, j)),
        out_shape=jax.ShapeDtypeStruct((rows, n), F32),
        compiler_params=pltpu.CompilerParams(vmem_limit_bytes=VMEM_LIMIT),
        name="ada",
    )(c_pad, w, b)


def _inproj_kernel(x_ref, sc_ref, sh_ref, wa_ref, wz_ref, wba_ref, wb_ref, wg_ref,
                   oa_ref, oz_ref, oba_ref, ob_ref, og_ref):
    h = (x_ref[...] * (1.0 + sc_ref[...]) + sh_ref[...]).astype(BF16)
    tn = 512
    for w_ref, o_ref in ((wa_ref, oa_ref), (wz_ref, oz_ref), (wb_ref, ob_ref), (wg_ref, og_ref)):
        for j in range(w_ref.shape[1] // tn):
            o_ref[:, j * tn:(j + 1) * tn] = jnp.dot(
                h, w_ref[:, j * tn:(j + 1) * tn], preferred_element_type=F32).astype(o_ref.dtype)
    oba_ref[...] = jnp.dot(h, wba_ref[...], preferred_element_type=F32)


def _inproj(x, scale, shift, wa, wz, wba, wb, wg, tm):
    b_, s_, d = x.shape
    tok = lambda n: pl.BlockSpec((None, tm, n), lambda b, i: (b, i, 0))
    vec = pl.BlockSpec((None, 1, d), lambda b, i: (b, 0, 0))
    outs = [(wa.shape[1], BF16), (wz.shape[1], BF16), (wba.shape[1], F32),
            (wb.shape[1], BF16), (wg.shape[1], BF16)]
    return pl.pallas_call(
        _inproj_kernel,
        grid=(b_, s_ // tm),
        in_specs=[tok(d), vec, vec] + [_const_spec(w.shape) for w in (wa, wz, wba, wb, wg)],
        out_specs=[tok(n) for n, _ in outs],
        out_shape=[jax.ShapeDtypeStruct((b_, s_, n), dt) for n, dt in outs],
        compiler_params=pltpu.CompilerParams(
            dimension_semantics=("parallel", "parallel"), vmem_limit_bytes=VMEM_LIMIT),
        name="inproj",
    )(x, scale, shift, wa, wz, wba, wb, wg)


def _unit_lower_inverse(a_strict):
    n = a_strict.shape[0]
    row = lax.broadcasted_iota(jnp.int32, (n, n), 0)
    col = lax.broadcasted_iota(jnp.int32, (n, n), 1)
    lower = row > col
    t = jnp.where(row == col, 1.0, 0.0) - jnp.where(((row ^ col) == 1) & lower, a_strict, 0.0)
    for k in range(1, int(math.log2(n))):
        b_k = jnp.where((((row >> k) ^ (col >> k)) == 1) & lower, a_strict, 0.0)
        t = t - _mm(t, _mm(b_k, t))
    return t


def _deltanet_kernel(qkv_ref, z_ref, ba_ref, cw_ref, alog_ref, dtb_ref, nw_ref, tri_ref, o_ref,
                     xbuf, state, *, ts):
    s = pl.program_id(1)
    halo = SUBLANES

    @pl.when(s == 0)
    def _():
        xbuf[0:halo, :] = jnp.zeros((halo, xbuf.shape[1]), F32)
        state[...] = jnp.zeros(state.shape, F32)

    xbuf[halo:halo + ts, :] = qkv_ref[...].astype(F32)

    ba = ba_ref[...]
    beta = _sigmoid(ba)
    g = -jnp.exp(alog_ref[...]) * _softplus(ba + dtb_ref[...])
    gc = jnp.dot(tri_ref[...], g, preferred_element_type=F32, precision=lax.Precision.HIGHEST)
    beta_t = beta.T
    gc_t = gc.T
    eg = jnp.exp(gc)
    eg_t = jnp.exp(gc_t)

    row = lax.broadcasted_iota(jnp.int32, (CHUNK, CHUNK), 0)
    col = lax.broadcasted_iota(jnp.int32, (CHUNK, CHUNK), 1)
    causal = row >= col
    strict = row > col

    def conv_silu(c0):
        acc = cw_ref[0:1, c0:c0 + LANES] * xbuf[halo - 3:halo - 3 + ts, c0:c0 + LANES]
        for k in range(1, A_CONV):
            acc = acc + cw_ref[k:k + 1, c0:c0 + LANES] * xbuf[halo - 3 + k:halo - 3 + k + ts, c0:c0 + LANES]
        return _silu(acc)

    def l2n(t):
        return t * lax.rsqrt(jnp.sum(t * t, axis=-1, keepdims=True) + L2_EPS)

    for h in range(A_HEADS):
        qh = l2n(conv_silu(h * A_DK)) * (A_DK ** -0.5)
        kh = l2n(conv_silu(A_W + h * A_DK))
        vh = conv_silu(2 * A_W + h * A_DV)
        kh_t = kh.T
        zh = z_ref[:, h * A_DV:(h + 1) * A_DV].astype(F32)
        st = state[h]
        for c in range(ts // CHUNK):
            r0, r1 = c * CHUNK, (c + 1) * CHUNK
            q_c, k_c, v_c, kt_c = qh[r0:r1], kh[r0:r1], vh[r0:r1], kh_t[:, r0:r1]
            g_col = gc[r0:r1, A_HEADS + h:A_HEADS + h + 1]
            g_row = gc_t[A_HEADS + h:A_HEADS + h + 1, r0:r1]
            b_col = beta[r0:r1, h:h + 1]
            b_row = beta_t[h:h + 1, r0:r1]
            eg_col = eg[r0:r1, A_HEADS + h:A_HEADS + h + 1]
            eg_row = eg_t[A_HEADS + h:A_HEADS + h + 1, r0:r1]
            diff = g_col - g_row
            decay = jnp.where(causal, jnp.exp(jnp.where(causal, diff, 0.0)), 0.0)
            kk = _mm(k_c, kt_c)
            qk = _mm(q_c, kt_c) * decay
            a_low = jnp.where(strict, kk * b_col * decay, 0.0)
            t_inv = _unit_lower_inverse(a_low)
            t_u = t_inv * b_row
            t_w = t_u * eg_row
            u = _mm(t_u, v_c)
            w = _mm(t_w, k_c)
            v_new = u - _mm(w, st)
            o = eg_col * _mm(q_c, st) + _mm(qk, v_new)
            g_last = g_row[:, CHUNK - 1:CHUNK]
            kd_t = kt_c * jnp.exp(g_last - g_row)
            st = st * jnp.exp(g_last) + _mm(kd_t, v_new)
            o = o * lax.rsqrt(jnp.mean(o * o, axis=-1, keepdims=True) + RMS_EPS)
            o = o * nw_ref[...] * _silu(zh[r0:r1])
            o_ref[r0:r1, h * A_DV:(h + 1) * A_DV] = o.astype(o_ref.dtype)
        state[h] = st

    xbuf[0:halo, :] = xbuf[ts:ts + halo, :]


def _deltanet(qkv, z, ba, conv_w, alog, dtb, norm_w, ts):
    b_, s_, _ = qkv.shape
    tri = np.tril(np.ones((CHUNK, CHUNK), np.float32))
    tri = jnp.asarray(np.kron(np.eye(ts // CHUNK, dtype=np.float32), tri))
    tok = lambda n: pl.BlockSpec((None, ts, n), lambda b, i: (b, i, 0))
    return pl.pallas_call(
        functools.partial(_deltanet_kernel, ts=ts),
        grid=(b_, s_ // ts),
        in_specs=[tok(3 * A_W), tok(A_W), tok(LANES), _const_spec(conv_w.shape),
                  _const_spec(alog.shape), _const_spec(dtb.shape), _const_spec(norm_w.shape),
                  _const_spec(tri.shape)],
        out_specs=tok(A_W),
        out_shape=jax.ShapeDtypeStruct((b_, s_, A_W), BF16),
        scratch_shapes=[pltpu.VMEM((ts + SUBLANES, 3 * A_W), F32),
                        pltpu.VMEM((A_HEADS, A_DK, A_DV), F32)],
        compiler_params=pltpu.CompilerParams(
            dimension_semantics=("parallel", "arbitrary"), vmem_limit_bytes=VMEM_LIMIT),
        name="deltanet",
    )(qkv, z, ba, conv_w, alog, dtb, norm_w, tri)


def _bandattn_kernel(q_ref, kp_ref, kc_ref, vp_ref, vc_ref, bias_ref, o_ref, kcat, vcat, *, tq):
    i = pl.program_id(1)
    kcat[0:tq, :] = kp_ref[...]
    kcat[tq:2 * tq, :] = kc_ref[...]
    vcat[0:tq, :] = vp_ref[...]
    vcat[tq:2 * tq, :] = vc_ref[...]
    scale = B_DH ** -0.5
    lane = lax.broadcasted_iota(jnp.int32, (CHUNK, LANES), 1)
    low_half = lane < B_DH
    kpos = lax.broadcasted_iota(jnp.int32, (CHUNK, BAND), 1)

    def chunk_body(c, carry):
        r0 = pl.multiple_of(c * CHUNK, CHUNK)
        valid = (i * tq + c * CHUNK - B_PREV_CHUNKS * CHUNK + kpos) >= 0
        for p in range(B_HEADS // 2):
            c0 = p * LANES
            q_pair = q_ref[pl.ds(r0, CHUNK), c0:c0 + LANES]
            k_band = kcat[pl.ds(r0, BAND), c0:c0 + LANES]
            v_band = vcat[pl.ds(r0, BAND), c0:c0 + LANES]
            outs = []
            for hh in range(2):
                keep = low_half if hh == 0 else jnp.logical_not(low_half)
                qm = jnp.where(keep, q_pair, jnp.zeros_like(q_pair))
                sc = _mm_nt(qm, k_band) * scale + bias_ref[2 * p + hh]
                sc = jnp.where(valid, sc, NEG_INF)
                m = jnp.max(sc, axis=-1, keepdims=True)
                e = jnp.exp(sc - m)
                den = jnp.sum(e, axis=-1, keepdims=True)
                outs.append(_mm(e, v_band) * (1.0 / den))
            o_ref[pl.ds(r0, CHUNK), c0:c0 + LANES] = jnp.where(low_half, outs[0], outs[1]).astype(o_ref.dtype)
        return carry

    lax.fori_loop(0, tq // CHUNK, chunk_body, 0)


def _bandattn(qkv, bias, tq):
    b_, s_, _ = qkv.shape
    nb = B_W // B_W
    q_spec = pl.BlockSpec((None, tq, B_W), lambda b, i: (b, i, 0))
    kp_spec = pl.BlockSpec((None, tq, B_W), lambda b, i: (b, jnp.maximum(i - 1, 0), nb))
    kc_spec = pl.BlockSpec((None, tq, B_W), lambda b, i: (b, i, nb))
    vp_spec = pl.BlockSpec((None, tq, B_W), lambda b, i: (b, jnp.maximum(i - 1, 0), 2 * nb))
    vc_spec = pl.BlockSpec((None, tq, B_W), lambda b, i: (b, i, 2 * nb))
    return pl.pallas_call(
        functools.partial(_bandattn_kernel, tq=tq),
        grid=(b_, s_ // tq),
        in_specs=[q_spec, kp_spec, kc_spec, vp_spec, vc_spec, _const_spec(bias.shape)],
        out_specs=pl.BlockSpec((None, tq, B_W), lambda b, i: (b, i, 0)),
        out_shape=jax.ShapeDtypeStruct((b_, s_, B_W), BF16),
        scratch_shapes=[pltpu.VMEM((2 * tq, B_W), BF16), pltpu.VMEM((2 * tq, B_W), BF16)],
        compiler_params=pltpu.CompilerParams(
            dimension_semantics=("parallel", "arbitrary"), vmem_limit_bytes=VMEM_LIMIT),
        name="bandattn",
    )(qkv, qkv, qkv, qkv, qkv, bias)


def _merge_kernel(x_ref, oa_ref, ob_ref, g_ref, bg_ref, gt_ref, wa_ref, wb_ref, wo_ref,
                  lg_ref, lb_ref, o_ref, *, alpha):
    d = x_ref.shape[-1]
    gates = _sigmoid(g_ref[...].astype(F32) + bg_ref[...])
    merged = (gates[:, :d] * jnp.dot(oa_ref[...], wa_ref[...], preferred_element_type=F32)
              + gates[:, d:] * jnp.dot(ob_ref[...], wb_ref[...], preferred_element_type=F32))
    mix = jnp.dot(merged.astype(BF16), wo_ref[...], preferred_element_type=F32)
    y = alpha * x_ref[...] + gt_ref[...] * mix
    o_ref[...] = _layernorm(y, lg_ref[...], lb_ref[...])


def _merge(x, o_a, o_b, gates, b_gate, gate_t, wa, wb, wo, ln_g, ln_b, alpha, tm):
    b_, s_, d = x.shape
    tok = lambda n: pl.BlockSpec((None, tm, n), lambda b, i: (b, i, 0))
    vec = pl.BlockSpec((None, 1, d), lambda b, i: (b, 0, 0))
    return pl.pallas_call(
        functools.partial(_merge_kernel, alpha=alpha),
        grid=(b_, s_ // tm),
        in_specs=[tok(d), tok(A_W), tok(B_W), tok(2 * d), _const_spec(b_gate.shape), vec,
                  _const_spec(wa.shape), _const_spec(wb.shape), _const_spec(wo.shape),
                  _const_spec(ln_g.shape), _const_spec(ln_b.shape)],
        out_specs=tok(d),
        out_shape=jax.ShapeDtypeStruct((b_, s_, d), F32),
        compiler_params=pltpu.CompilerParams(
            dimension_semantics=("parallel", "parallel"), vmem_limit_bytes=VMEM_LIMIT),
        name="merge",
    )(x, o_a, o_b, gates, b_gate, gate_t, wa, wb, wo, ln_g, ln_b)


def _ffn_kernel(x_ref, sc_ref, sh_ref, gf_ref, wu_ref, cw_ref, cb_ref, wd_ref, lg_ref, lb_ref,
                o_ref, ubuf, carry, *, alpha, ts, d_ff, tf):
    s = pl.program_id(1)
    halo = SUBLANES

    @pl.when(s == 0)
    def _():
        carry[...] = jnp.zeros(carry.shape, F32)

    x = x_ref[...]
    h = (x * (1.0 + sc_ref[...]) + sh_ref[...]).astype(BF16)

    def conv_half(c0):
        up = jnp.dot(h, wu_ref[:, c0:c0 + tf], preferred_element_type=F32)
        ubuf[0:halo, :] = carry[:, c0:c0 + tf]
        ubuf[halo:halo + ts, :] = up
        carry[:, c0:c0 + tf] = up[ts - halo:ts, :]
        acc = cb_ref[0:1, c0:c0 + tf] + cw_ref[0:1, c0:c0 + tf] * ubuf[halo - 2:halo - 2 + ts, :]
        for k in range(1, FFN_CONV):
            acc = acc + cw_ref[k:k + 1, c0:c0 + tf] * ubuf[halo - 2 + k:halo - 2 + k + ts, :]
        return acc

    ffn = jnp.zeros((ts, x.shape[-1]), F32)
    for j in range(d_ff // tf):
        u_gate = conv_half(j * tf)
        u_val = conv_half(d_ff + j * tf)
        act = (_silu(u_gate) * u_val).astype(BF16)
        ffn = ffn + jnp.dot(act, wd_ref[j * tf:(j + 1) * tf, :], preferred_element_type=F32)
    y = alpha * x + gf_ref[...] * ffn
    o_ref[...] = _layernorm(y, lg_ref[...], lb_ref[...])


def _ffn(x, scale, shift, gate_f, wu, cw, cb, wd, ln_g, ln_b, alpha, ts, tf):
    b_, s_, d = x.shape
    d_ff = wd.shape[0]
    tok = pl.BlockSpec((None, ts, d), lambda b, i: (b, i, 0))
    vec = pl.BlockSpec((None, 1, d), lambda b, i: (b, 0, 0))
    return pl.pallas_call(
        functools.partial(_ffn_kernel, alpha=alpha, ts=ts, d_ff=d_ff, tf=tf),
        grid=(b_, s_ // ts),
        in_specs=[tok, vec, vec, vec, _const_spec(wu.shape), _const_spec(cw.shape),
                  _const_spec(cb.shape), _const_spec(wd.shape), _const_spec(ln_g.shape),
                  _const_spec(ln_b.shape)],
        out_specs=tok,
        out_shape=jax.ShapeDtypeStruct((b_, s_, d), F32),
        scratch_shapes=[pltpu.VMEM((ts + SUBLANES, tf), F32), pltpu.VMEM((SUBLANES, 2 * d_ff), F32)],
        compiler_params=pltpu.CompilerParams(
            dimension_semantics=("parallel", "arbitrary"), vmem_limit_bytes=VMEM_LIMIT),
        name="ffn",
    )(x, scale, shift, gate_f, wu, cw, cb, wd, ln_g, ln_b)


def _pad_lanes(v, offset):
    return jnp.zeros((1, LANES), F32).at[0, offset:offset + v.shape[0]].set(v)


def kernel(x, c, w_ada, b_ada, w_in, b_gate, conv_a, a_log, dt_bias, norm_a, rel_bias,
           w_branch_a, w_branch_b, w_o, ln1_g, ln1_b, w_up, conv_ffn, b_conv_ffn, w_down,
           ln2_g, ln2_b):
    b_, s_, d = x.shape
    depth = w_ada.shape[0]
    alpha = (2.0 * depth) ** 0.25
    c_pad = jnp.zeros((SUBLANES, d), F32).at[:b_].set(c)

    pad = B_PREV_CHUNKS * CHUNK
    dist = pad + np.arange(CHUNK)[:, None] - np.arange(BAND)[None, :]
    rel_idx = np.clip(dist, -(CHUNK - 1), B_MAX_REL) + (CHUNK - 1)

    o0, o1, o2, o3, o4, o5, o6 = (3 * A_W, 4 * A_W, 4 * A_W + A_HEADS, 4 * A_W + 2 * A_HEADS,
                                  4 * A_W + 2 * A_HEADS + B_W, 4 * A_W + 2 * A_HEADS + 2 * B_W,
                                  4 * A_W + 2 * A_HEADS + 3 * B_W)
    for l in range(depth):
        mod = _ada(c_pad, w_ada[l], b_ada[l][None, :])[:b_]
        shift_t, scale_t, gate_t, shift_f, scale_f, gate_f = [
            m[:, None, :] for m in jnp.split(mod, 6, axis=-1)]

        wi = w_in[l]
        w_qkva = wi[:, :o0].astype(BF16)
        w_z = wi[:, o0:o1].astype(BF16)
        w_ba = jnp.zeros((d, LANES), BF16).at[:, :2 * A_HEADS].set(wi[:, o1:o3].astype(BF16))
        w_qkvb = wi[:, o3:o6].astype(BF16)
        w_g = wi[:, o6:].astype(BF16)
        qkva, z_a, ba, qkvb, gates = _inproj(x, scale_t, shift_t, w_qkva, w_z, w_ba, w_qkvb, w_g, tm=512)

        o_a = _deltanet(qkva, z_a, ba, conv_a[l], _pad_lanes(a_log[l], A_HEADS),
                        _pad_lanes(dt_bias[l], A_HEADS), norm_a[l][None, :], ts=128)
        bias = rel_bias[l][:, rel_idx]
        o_b = _bandattn(qkvb, bias, tq=512)

        x = _merge(x, o_a, o_b, gates, b_gate[l][None, :], gate_t, w_branch_a[l].astype(BF16),
                   w_branch_b[l].astype(BF16), w_o[l].astype(BF16), ln1_g[l][None, :],
                   ln1_b[l][None, :], alpha, tm=512)
        x = _ffn(x, scale_f, shift_f, gate_f, w_up[l].astype(BF16), conv_ffn[l],
                 b_conv_ffn[l][None, :], w_down[l].astype(BF16), ln2_g[l][None, :],
                 ln2_b[l][None, :], alpha, ts=512, tf=256)
    return x
```

```python
import functools
import math

import jax
import jax.numpy as jnp
import numpy as np
from jax import lax
from jax.experimental import pallas as pl
from jax.experimental.pallas import tpu as pltpu

F32 = jnp.float32
BF16 = jnp.bfloat16

CHUNK = 64
A_HEADS = 8
A_DK = 128
A_DV = 128
A_CONV = 4
A_W = A_HEADS * A_DV
B_HEADS = 16
B_DH = 64
B_W = B_HEADS * B_DH
B_PREV_CHUNKS = 8
B_MAX_REL = 256
BAND = (B_PREV_CHUNKS + 1) * CHUNK
FFN_CONV = 3
LN_EPS = 1e-5
RMS_EPS = 1e-6
L2_EPS = 1e-6
NEG_INF = -1e30

LANES = 128
SUBLANES = 8
VMEM_LIMIT = 56 * 1024 * 1024


def _mm(a, b):
    return jnp.dot(a.astype(BF16), b.astype(BF16), preferred_element_type=F32)


def _mm_nt(a, b):
    return lax.dot_general(a.astype(BF16), b.astype(BF16), (((1,), (1,)), ((), ())),
                           preferred_element_type=F32)


def _sigmoid(x):
    return 1.0 / (1.0 + jnp.exp(-x))


def _silu(x):
    return x * _sigmoid(x)


def _softplus(x):
    return jnp.maximum(x, 0.0) + jnp.log(1.0 + jnp.exp(-jnp.abs(x)))


def _layernorm(y, g, b):
    mu = jnp.mean(y, axis=-1, keepdims=True)
    d = y - mu
    var = jnp.mean(d * d, axis=-1, keepdims=True)
    return d * lax.rsqrt(var + LN_EPS) * g + b


def _const_spec(shape):
    return pl.BlockSpec(shape, lambda *_: (0,) * len(shape), pipeline_mode=pl.Buffered(1))


def _ada_kernel(c_ref, w_ref, b_ref, o_ref):
    c = c_ref[...]
    o_ref[...] = jnp.dot(_silu(c), w_ref[...], preferred_element_type=F32,
                         precision=lax.Precision.HIGHEST) + b_ref[...]


def _ada(c_pad, w, b):
    rows, d = c_pad.shape
    n = w.shape[1]
    tn = 1536
    return pl.pallas_call(
        _ada_kernel,
        grid=(n // tn,),
        in_specs=[pl.BlockSpec((rows, d), lambda j: (0, 0)),
                  pl.BlockSpec((d, tn), lambda j: (0, j)),
                  pl.BlockSpec((1, tn), lambda j: (0, j))],
        out_specs=pl.BlockSpec((rows, tn), lambda j: (0, j)),
        out_shape=jax.ShapeDtypeStruct((rows, n), F32),
        compiler_params=pltpu.CompilerParams(vmem_limit_bytes=VMEM_LIMIT),
        name="ada",
    )(c_pad, w, b)


def _inproj_kernel(x_ref, sc_ref, sh_ref, wa_ref, wz_ref, wba_ref, wb_ref, wg_ref,
                   oa_ref, oz_ref, oba_ref, ob_ref, og_ref):
    h = (x_ref[...] * (1.0 + sc_ref[...]) + sh_ref[...]).astype(BF16)
    tn = 512
    for w_ref, o_ref in ((wa_ref, oa_ref), (wz_ref, oz_ref), (wb_ref, ob_ref), (wg_ref, og_ref)):
        for j in range(w_ref.shape[1] // tn):
            o_ref[:, j * tn:(j + 1) * tn] = jnp.dot(
                h, w_ref[:, j * tn:(j + 1) * tn], preferred_element_type=F32).astype(o_ref.dtype)
    oba_ref[...] = jnp.dot(h, wba_ref[...], preferred_element_type=F32)


def _inproj(x, scale, shift, wa, wz, wba, wb, wg, tm):
    b_, s_, d = x.shape
    tok = lambda n: pl.BlockSpec((None, tm, n), lambda b, i: (b, i, 0))
    vec = pl.BlockSpec((None, 1, d), lambda b, i: (b, 0, 0))
    outs = [(wa.shape[1], BF16), (wz.shape[1], BF16), (wba.shape[1], F32),
            (wb.shape[1], BF16), (wg.shape[1], BF16)]
    return pl.pallas_call(
        _inproj_kernel,
        grid=(b_, s_ // tm),
        in_specs=[tok(d), vec, vec] + [_const_spec(w.shape) for w in (wa, wz, wba, wb, wg)],
        out_specs=[tok(n) for n, _ in outs],
        out_shape=[jax.ShapeDtypeStruct((b_, s_, n), dt) for n, dt in outs],
        compiler_params=pltpu.CompilerParams(
            dimension_semantics=("parallel", "parallel"), vmem_limit_bytes=VMEM_LIMIT),
        name="inproj",
    )(x, scale, shift, wa, wz, wba, wb, wg)


PAIR = 2 * CHUNK


def _deltanet_kernel(qkv_ref, z_ref, ba_ref, cw_ref, alog_ref, dtb_ref, nw_ref, tri_ref, o_ref,
                     xbuf, state, *, ts):
    s = pl.program_id(1)
    halo = SUBLANES

    @pl.when(s == 0)
    def _():
        xbuf[0:halo, :] = jnp.zeros((halo, xbuf.shape[1]), F32)
        state[...] = jnp.zeros(state.shape, F32)

    xbuf[halo:halo + ts, :] = qkv_ref[...].astype(F32)

    ba = ba_ref[...]
    beta = _sigmoid(ba)
    g = -jnp.exp(alog_ref[...]) * _softplus(ba + dtb_ref[...])
    gc = jnp.dot(tri_ref[...], g, preferred_element_type=F32, precision=lax.Precision.HIGHEST)
    beta_t = beta.T
    gc_t = gc.T
    eg = jnp.exp(gc)
    eg_t = jnp.exp(gc_t)

    row = lax.broadcasted_iota(jnp.int32, (PAIR, PAIR), 0)
    col = lax.broadcasted_iota(jnp.int32, (PAIR, PAIR), 1)
    same = (row // CHUNK) == (col // CHUNK)
    causal = same & (row >= col)
    strict = same & (row > col)
    eye = jnp.where(row == col, 1.0, 0.0)
    n_levels = int(math.log2(CHUNK))
    level_mask = [(((row >> k) ^ (col >> k)) == 1) & (row > col) for k in range(n_levels)]

    def conv_silu(c0):
        acc = cw_ref[0:1, c0:c0 + LANES] * xbuf[halo - 3:halo - 3 + ts, c0:c0 + LANES]
        for k in range(1, A_CONV):
            acc = acc + cw_ref[k:k + 1, c0:c0 + LANES] * xbuf[halo - 3 + k:halo - 3 + k + ts, c0:c0 + LANES]
        return _silu(acc)

    def l2n(t):
        return t * lax.rsqrt(jnp.sum(t * t, axis=-1, keepdims=True) + L2_EPS)

    heads = range(A_HEADS)
    n_pairs = ts // PAIR
    q_b, k_b, v_b, k_t = [], [], [], []
    for h in heads:
        q_b.append((l2n(conv_silu(h * A_DK)) * (A_DK ** -0.5)).astype(BF16))
        kh = l2n(conv_silu(A_W + h * A_DK))
        k_b.append(kh.astype(BF16))
        k_t.append(kh.T)
        v_b.append(conv_silu(2 * A_W + h * A_DV).astype(BF16))

    units = [(h, d) for h in heads for d in range(n_pairs)]

    def rows(d):
        return slice(d * PAIR, (d + 1) * PAIR)

    def g_row_of(h, d):
        return gc_t[A_HEADS + h:A_HEADS + h + 1, rows(d)]

    decay = []
    for h, d in units:
        diff = gc[rows(d), A_HEADS + h:A_HEADS + h + 1] - g_row_of(h, d)
        decay.append(jnp.where(causal, jnp.exp(jnp.where(causal, diff, 0.0)), 0.0))
    kk = [_mm(k_b[h][rows(d)], k_t[h][:, rows(d)]) for h, d in units]
    qk = [(_mm(q_b[h][rows(d)], k_t[h][:, rows(d)]) * decay[j]).astype(BF16)
          for j, (h, d) in enumerate(units)]
    a_low = [jnp.where(strict, kk[j] * beta[rows(d), h:h + 1] * decay[j], 0.0)
             for j, (h, d) in enumerate(units)]
    t_inv = [eye - jnp.where(level_mask[0], a, 0.0) for a in a_low]
    for k in range(1, n_levels):
        x = [_mm(jnp.where(level_mask[k], a_low[j], 0.0), t_inv[j]) for j in range(len(units))]
        t_inv = [t_inv[j] - _mm(t_inv[j], x[j]) for j in range(len(units))]
    t_u = [t_inv[j] * beta_t[h:h + 1, rows(d)] for j, (h, d) in enumerate(units)]
    u = [_mm(t_u[j], v_b[h][rows(d)]) for j, (h, d) in enumerate(units)]
    w = [_mm(t_u[j] * eg_t[A_HEADS + h:A_HEADS + h + 1, rows(d)], k_b[h][rows(d)]).astype(BF16)
         for j, (h, d) in enumerate(units)]

    st = [state[h] for h in heads]
    lane_hi = lax.broadcasted_iota(jnp.int32, (1, PAIR), 1) >= CHUNK
    for d in range(n_pairs):
        v_new_lo = [None] * A_HEADS
        for half in range(2):
            r0 = d * PAIR + half * CHUNK
            sub = slice(half * CHUNK, (half + 1) * CHUNK)
            ws_qs = [_mm(jnp.concatenate([w[h * n_pairs + d][sub], q_b[h][r0:r0 + CHUNK]], axis=0), st[h])
                     for h in heads]
            v_new_b = [(u[h * n_pairs + d][sub] - ws_qs[h][:CHUNK]).astype(BF16) for h in heads]
            o_l = []
            for h in heads:
                j = h * n_pairs + d
                g_row = g_row_of(h, d)
                g_last = g_row[:, (half + 1) * CHUNK - 1:(half + 1) * CHUNK]
                e_row = jnp.exp(g_last - g_row)
                eg_col = eg[r0:r0 + CHUNK, A_HEADS + h:A_HEADS + h + 1]
                if half == 0:
                    rhs = v_new_b[h]
                    qk_c = qk[j][sub, 0:CHUNK]
                    kd_t = k_t[h][:, d * PAIR:d * PAIR + CHUNK] * e_row[:, 0:CHUNK]
                else:
                    rhs = jnp.concatenate([v_new_lo[h], v_new_b[h]], axis=0)
                    qk_c = qk[j][sub, :]
                    kd_t = k_t[h][:, rows(d)] * jnp.where(lane_hi, e_row, 0.0)
                o_l.append(eg_col * ws_qs[h][CHUNK:] + _mm(qk_c, rhs))
                st[h] = st[h] * jnp.exp(g_last) + _mm(kd_t, rhs)
            for h in heads:
                o = o_l[h]
                o = o * lax.rsqrt(jnp.mean(o * o, axis=-1, keepdims=True) + RMS_EPS)
                zh = z_ref[r0:r0 + CHUNK, h * A_DV:(h + 1) * A_DV].astype(F32)
                o_ref[r0:r0 + CHUNK, h * A_DV:(h + 1) * A_DV] = (o * nw_ref[...] * _silu(zh)).astype(o_ref.dtype)
            v_new_lo = v_new_b
    for h in heads:
        state[h] = st[h]

    xbuf[0:halo, :] = xbuf[ts:ts + halo, :]


def _deltanet(qkv, z, ba, conv_w, alog, dtb, norm_w, ts):
    b_, s_, _ = qkv.shape
    tri = np.tril(np.ones((CHUNK, CHUNK), np.float32))
    tri = jnp.asarray(np.kron(np.eye(ts // CHUNK, dtype=np.float32), tri))
    tok = lambda n: pl.BlockSpec((None, ts, n), lambda b, i: (b, i, 0))
    return pl.pallas_call(
        functools.partial(_deltanet_kernel, ts=ts),
        grid=(b_, s_ // ts),
        in_specs=[tok(3 * A_W), tok(A_W), tok(LANES), _const_spec(conv_w.shape),
                  _const_spec(alog.shape), _const_spec(dtb.shape), _const_spec(norm_w.shape),
                  _const_spec(tri.shape)],
        out_specs=tok(A_W),
        out_shape=jax.ShapeDtypeStruct((b_, s_, A_W), BF16),
        scratch_shapes=[pltpu.VMEM((ts + SUBLANES, 3 * A_W), F32),
                        pltpu.VMEM((A_HEADS, A_DK, A_DV), F32)],
        compiler_params=pltpu.CompilerParams(
            dimension_semantics=("parallel", "arbitrary"), vmem_limit_bytes=VMEM_LIMIT),
        name="deltanet",
    )(qkv, z, ba, conv_w, alog, dtb, norm_w, tri)


def _bandattn_kernel(q_ref, kp_ref, kc_ref, vp_ref, vc_ref, bias_ref, o_ref, kcat, vcat, *, tq):
    i = pl.program_id(1)
    kcat[0:tq, :] = kp_ref[...]
    kcat[tq:2 * tq, :] = kc_ref[...]
    vcat[0:tq, :] = vp_ref[...]
    vcat[tq:2 * tq, :] = vc_ref[...]
    scale = B_DH ** -0.5
    lane = lax.broadcasted_iota(jnp.int32, (CHUNK, LANES), 1)
    low_half = lane < B_DH
    kpos = lax.broadcasted_iota(jnp.int32, (CHUNK, BAND), 1)

    def chunk_body(c, carry):
        r0 = pl.multiple_of(c * CHUNK, CHUNK)
        valid = (i * tq + c * CHUNK - B_PREV_CHUNKS * CHUNK + kpos) >= 0
        for p in range(B_HEADS // 2):
            c0 = p * LANES
            q_pair = q_ref[pl.ds(r0, CHUNK), c0:c0 + LANES]
            k_band = kcat[pl.ds(r0, BAND), c0:c0 + LANES]
            v_band = vcat[pl.ds(r0, BAND), c0:c0 + LANES]
            outs = []
            for hh in range(2):
                keep = low_half if hh == 0 else jnp.logical_not(low_half)
                qm = jnp.where(keep, q_pair, jnp.zeros_like(q_pair))
                sc = _mm_nt(qm, k_band) * scale + bias_ref[2 * p + hh]
                sc = jnp.where(valid, sc, NEG_INF)
                m = jnp.max(sc, axis=-1, keepdims=True)
                e = jnp.exp(sc - m)
                den = jnp.sum(e, axis=-1, keepdims=True)
                outs.append(_mm(e, v_band) * (1.0 / den))
            o_ref[pl.ds(r0, CHUNK), c0:c0 + LANES] = jnp.where(low_half, outs[0], outs[1]).astype(o_ref.dtype)
        return carry

    lax.fori_loop(0, tq // CHUNK, chunk_body, 0)


def _bandattn(qkv, bias, tq):
    b_, s_, _ = qkv.shape
    nb = B_W // B_W
    q_spec = pl.BlockSpec((None, tq, B_W), lambda b, i: (b, i, 0))
    kp_spec = pl.BlockSpec((None, tq, B_W), lambda b, i: (b, jnp.maximum(i - 1, 0), nb))
    kc_spec = pl.BlockSpec((None, tq, B_W), lambda b, i: (b, i, nb))
    vp_spec = pl.BlockSpec((None, tq, B_W), lambda b, i: (b, jnp.maximum(i - 1, 0), 2 * nb))
    vc_spec = pl.BlockSpec((None, tq, B_W), lambda b, i: (b, i, 2 * nb))
    return pl.pallas_call(
        functools.partial(_bandattn_kernel, tq=tq),
        grid=(b_, s_ // tq),
        in_specs=[q_spec, kp_spec, kc_spec, vp_spec, vc_spec, _const_spec(bias.shape)],
        out_specs=pl.BlockSpec((None, tq, B_W), lambda b, i: (b, i, 0)),
        out_shape=jax.ShapeDtypeStruct((b_, s_, B_W), BF16),
        scratch_shapes=[pltpu.VMEM((2 * tq, B_W), BF16), pltpu.VMEM((2 * tq, B_W), BF16)],
        compiler_params=pltpu.CompilerParams(
            dimension_semantics=("parallel", "arbitrary"), vmem_limit_bytes=VMEM_LIMIT),
        name="bandattn",
    )(qkv, qkv, qkv, qkv, qkv, bias)


def _merge_kernel(x_ref, oa_ref, ob_ref, g_ref, bg_ref, gt_ref, wa_ref, wb_ref, wo_ref,
                  lg_ref, lb_ref, o_ref, *, alpha):
    d = x_ref.shape[-1]
    gates = _sigmoid(g_ref[...].astype(F32) + bg_ref[...])
    merged = (gates[:, :d] * jnp.dot(oa_ref[...], wa_ref[...], preferred_element_type=F32)
              + gates[:, d:] * jnp.dot(ob_ref[...], wb_ref[...], preferred_element_type=F32))
    mix = jnp.dot(merged.astype(BF16), wo_ref[...], preferred_element_type=F32)
    y = alpha * x_ref[...] + gt_ref[...] * mix
    o_ref[...] = _layernorm(y, lg_ref[...], lb_ref[...])


def _merge(x, o_a, o_b, gates, b_gate, gate_t, wa, wb, wo, ln_g, ln_b, alpha, tm):
    b_, s_, d = x.shape
    tok = lambda n: pl.BlockSpec((None, tm, n), lambda b, i: (b, i, 0))
    vec = pl.BlockSpec((None, 1, d), lambda b, i: (b, 0, 0))
    return pl.pallas_call(
        functools.partial(_merge_kernel, alpha=alpha),
        grid=(b_, s_ // tm),
        in_specs=[tok(d), tok(A_W), tok(B_W), tok(2 * d), _const_spec(b_gate.shape), vec,
                  _const_spec(wa.shape), _const_spec(wb.shape), _const_spec(wo.shape),
                  _const_spec(ln_g.shape), _const_spec(ln_b.shape)],
        out_specs=tok(d),
        out_shape=jax.ShapeDtypeStruct((b_, s_, d), F32),
        compiler_params=pltpu.CompilerParams(
            dimension_semantics=("parallel", "parallel"), vmem_limit_bytes=VMEM_LIMIT),
        name="merge",
    )(x, o_a, o_b, gates, b_gate, gate_t, wa, wb, wo, ln_g, ln_b)


def _ffn_kernel(x_ref, sc_ref, sh_ref, gf_ref, wu_ref, cw_ref, cb_ref, wd_ref, lg_ref, lb_ref,
                o_ref, ubuf, carry, *, alpha, ts, d_ff, tf):
    s = pl.program_id(1)
    halo = SUBLANES

    @pl.when(s == 0)
    def _():
        carry[...] = jnp.zeros(carry.shape, F32)

    x = x_ref[...]
    h = (x * (1.0 + sc_ref[...]) + sh_ref[...]).astype(BF16)

    def conv_half(c0):
        up = jnp.dot(h, wu_ref[:, c0:c0 + tf], preferred_element_type=F32)
        ubuf[0:halo, :] = carry[:, c0:c0 + tf]
        ubuf[halo:halo + ts, :] = up
        carry[:, c0:c0 + tf] = up[ts - halo:ts, :]
        acc = cb_ref[0:1, c0:c0 + tf] + cw_ref[0:1, c0:c0 + tf] * ubuf[halo - 2:halo - 2 + ts, :]
        for k in range(1, FFN_CONV):
            acc = acc + cw_ref[k:k + 1, c0:c0 + tf] * ubuf[halo - 2 + k:halo - 2 + k + ts, :]
        return acc

    ffn = jnp.zeros((ts, x.shape[-1]), F32)
    for j in range(d_ff // tf):
        u_gate = conv_half(j * tf)
        u_val = conv_half(d_ff + j * tf)
        act = (_silu(u_gate) * u_val).astype(BF16)
        ffn = ffn + jnp.dot(act, wd_ref[j * tf:(j + 1) * tf, :], preferred_element_type=F32)
    y = alpha * x + gf_ref[...] * ffn
    o_ref[...] = _layernorm(y, lg_ref[...], lb_ref[...])


def _ffn(x, scale, shift, gate_f, wu, cw, cb, wd, ln_g, ln_b, alpha, ts, tf):
    b_, s_, d = x.shape
    d_ff = wd.shape[0]
    tok = pl.BlockSpec((None, ts, d), lambda b, i: (b, i, 0))
    vec = pl.BlockSpec((None, 1, d), lambda b, i: (b, 0, 0))
    return pl.pallas_call(
        functools.partial(_ffn_kernel, alpha=alpha, ts=ts, d_ff=d_ff, tf=tf),
        grid=(b_, s_ // ts),
        in_specs=[tok, vec, vec, vec, _const_spec(wu.shape), _const_spec(cw.shape),
                  _const_spec(cb.shape), _const_spec(wd.shape), _const_spec(ln_g.shape),
                  _const_spec(ln_b.shape)],
        out_specs=tok,
        out_shape=jax.ShapeDtypeStruct((b_, s_, d), F32),
        scratch_shapes=[pltpu.VMEM((ts + SUBLANES, tf), F32), pltpu.VMEM((SUBLANES, 2 * d_ff), F32)],
        compiler_params=pltpu.CompilerParams(
            dimension_semantics=("parallel", "arbitrary"), vmem_limit_bytes=VMEM_LIMIT),
        name="ffn",
    )(x, scale, shift, gate_f, wu, cw, cb, wd, ln_g, ln_b)


def _rel_bias_table(rel):
    n_rel = rel.shape[1]
    top = B_PREV_CHUNKS * CHUNK + 2 * (CHUNK - 1)
    ext = jnp.concatenate([rel, jnp.broadcast_to(rel[:, -1:], (rel.shape[0], top + 1 - n_rel))], axis=1)
    rev = ext[:, ::-1]
    return jnp.stack([rev[:, CHUNK - 1 - i:CHUNK - 1 - i + BAND] for i in range(CHUNK)], axis=1)


def _pad_lanes(v, offset):
    return jnp.zeros((1, LANES), F32).at[0, offset:offset + v.shape[0]].set(v)


def kernel(x, c, w_ada, b_ada, w_in, b_gate, conv_a, a_log, dt_bias, norm_a, rel_bias,
           w_branch_a, w_branch_b, w_o, ln1_g, ln1_b, w_up, conv_ffn, b_conv_ffn, w_down,
           ln2_g, ln2_b):
    b_, s_, d = x.shape
    depth = w_ada.shape[0]
    alpha = (2.0 * depth) ** 0.25
    c_pad = jnp.zeros((SUBLANES, d), F32).at[:b_].set(c)

    o0, o1, o2, o3, o4, o5, o6 = (3 * A_W, 4 * A_W, 4 * A_W + A_HEADS, 4 * A_W + 2 * A_HEADS,
                                  4 * A_W + 2 * A_HEADS + B_W, 4 * A_W + 2 * A_HEADS + 2 * B_W,
                                  4 * A_W + 2 * A_HEADS + 3 * B_W)
    for l in range(depth):
        mod = _ada(c_pad, w_ada[l], b_ada[l][None, :])[:b_]
        shift_t, scale_t, gate_t, shift_f, scale_f, gate_f = [
            m[:, None, :] for m in jnp.split(mod, 6, axis=-1)]

        wi = w_in[l]
        w_qkva = wi[:, :o0].astype(BF16)
        w_z = wi[:, o0:o1].astype(BF16)
        w_ba = jnp.zeros((d, LANES), BF16).at[:, :2 * A_HEADS].set(wi[:, o1:o3].astype(BF16))
        w_qkvb = wi[:, o3:o6].astype(BF16)
        w_g = wi[:, o6:].astype(BF16)
        qkva, z_a, ba, qkvb, gates = _inproj(x, scale_t, shift_t, w_qkva, w_z, w_ba, w_qkvb, w_g, tm=512)

        o_a = _deltanet(qkva, z_a, ba, conv_a[l], _pad_lanes(a_log[l], A_HEADS),
                        _pad_lanes(dt_bias[l], A_HEADS), norm_a[l][None, :], ts=256)
        bias = _rel_bias_table(rel_bias[l])
        o_b = _bandattn(qkvb, bias, tq=512)

        x = _merge(x, o_a, o_b, gates, b_gate[l][None, :], gate_t, w_branch_a[l].astype(BF16),
                   w_branch_b[l].astype(BF16), w_o[l].astype(BF16), ln1_g[l][None, :],
                   ln1_b[l][None, :], alpha, tm=512)
        x = _ffn(x, scale_f, shift_f, gate_f, w_up[l].astype(BF16), conv_ffn[l],
                 b_conv_ffn[l][None, :], w_down[l].astype(BF16), ln2_g[l][None, :],
                 ln2_b[l][None, :], alpha, ts=512, tf=256)
    return x
```

```python
import functools
import math

import jax
import jax.numpy as jnp
import numpy as np
from jax import lax
from jax.experimental import pallas as pl
from jax.experimental.pallas import tpu as pltpu

F32 = jnp.float32
BF16 = jnp.bfloat16

CHUNK = 64
A_HEADS = 8
A_DK = 128
A_DV = 128
A_CONV = 4
A_W = A_HEADS * A_DV
B_HEADS = 16
B_DH = 64
B_W = B_HEADS * B_DH
B_PREV_CHUNKS = 8
B_MAX_REL = 256
BAND = (B_PREV_CHUNKS + 1) * CHUNK
FFN_CONV = 3
LN_EPS = 1e-5
RMS_EPS = 1e-6
L2_EPS = 1e-6
NEG_INF = -1e30

LANES = 128
SUBLANES = 8
VMEM_LIMIT = 56 * 1024 * 1024


def _mm(a, b):
    return jnp.dot(a.astype(BF16), b.astype(BF16), preferred_element_type=F32)


def _mm_nt(a, b):
    return lax.dot_general(a.astype(BF16), b.astype(BF16), (((1,), (1,)), ((), ())),
                           preferred_element_type=F32)


def _sigmoid(x):
    return 1.0 / (1.0 + jnp.exp(-x))


def _silu(x):
    return x * _sigmoid(x)


def _softplus(x):
    return jnp.maximum(x, 0.0) + jnp.log(1.0 + jnp.exp(-jnp.abs(x)))


def _layernorm(y, g, b):
    mu = jnp.mean(y, axis=-1, keepdims=True)
    d = y - mu
    var = jnp.mean(d * d, axis=-1, keepdims=True)
    return d * lax.rsqrt(var + LN_EPS) * g + b


def _const_spec(shape):
    return pl.BlockSpec(shape, lambda *_: (0,) * len(shape), pipeline_mode=pl.Buffered(1))


def _ada_kernel(c_ref, w_ref, b_ref, o_ref):
    c = c_ref[...]
    o_ref[...] = jnp.dot(_silu(c), w_ref[...], preferred_element_type=F32,
                         precision=lax.Precision.HIGHEST) + b_ref[...]


def _ada(c_pad, w, b):
    rows, d = c_pad.shape
    n = w.shape[1]
    tn = 1536
    return pl.pallas_call(
        _ada_kernel,
        grid=(n // tn,),
        in_specs=[pl.BlockSpec((rows, d), lambda j: (0, 0)),
                  pl.BlockSpec((d, tn), lambda j: (0, j)),
                  pl.BlockSpec((1, tn), lambda j: (0, j))],
        out_specs=pl.BlockSpec((rows, tn), lambda j: (0, j)),
        out_shape=jax.ShapeDtypeStruct((rows, n), F32),
        compiler_params=pltpu.CompilerParams(vmem_limit_bytes=VMEM_LIMIT),
        name="ada",
    )(c_pad, w, b)


def _inproj_kernel(x_ref, sc_ref, sh_ref, wa_ref, wz_ref, wba_ref, wb_ref, wg_ref,
                   oa_ref, oz_ref, oba_ref, ob_ref, og_ref):
    h = (x_ref[...] * (1.0 + sc_ref[...]) + sh_ref[...]).astype(BF16)
    tn = 512
    for w_ref, o_ref in ((wa_ref, oa_ref), (wz_ref, oz_ref), (wb_ref, ob_ref), (wg_ref, og_ref)):
        for j in range(w_ref.shape[1] // tn):
            o_ref[:, j * tn:(j + 1) * tn] = jnp.dot(
                h, w_ref[:, j * tn:(j + 1) * tn], preferred_element_type=F32).astype(o_ref.dtype)
    oba_ref[...] = jnp.dot(h, wba_ref[...], preferred_element_type=F32)


def _inproj(x, scale, shift, wa, wz, wba, wb, wg, tm):
    b_, s_, d = x.shape
    tok = lambda n: pl.BlockSpec((None, tm, n), lambda b, i: (b, i, 0))
    vec = pl.BlockSpec((None, 1, d), lambda b, i: (b, 0, 0))
    outs = [(wa.shape[1], BF16), (wz.shape[1], BF16), (wba.shape[1], F32),
            (wb.shape[1], BF16), (wg.shape[1], BF16)]
    return pl.pallas_call(
        _inproj_kernel,
        grid=(b_, s_ // tm),
        in_specs=[tok(d), vec, vec] + [_const_spec(w.shape) for w in (wa, wz, wba, wb, wg)],
        out_specs=[tok(n) for n, _ in outs],
        out_shape=[jax.ShapeDtypeStruct((b_, s_, n), dt) for n, dt in outs],
        compiler_params=pltpu.CompilerParams(
            dimension_semantics=("parallel", "parallel"), vmem_limit_bytes=VMEM_LIMIT),
        name="inproj",
    )(x, scale, shift, wa, wz, wba, wb, wg)


PAIR = 2 * CHUNK


def _deltanet_kernel(qkv_ref, z_ref, ba_ref, cw_ref, alog_ref, dtb_ref, nw_ref, tri_ref, o_ref,
                     xbuf, state, *, ts):
    s = pl.program_id(1)
    halo = SUBLANES

    @pl.when(s == 0)
    def _():
        xbuf[0:halo, :] = jnp.zeros((halo, xbuf.shape[1]), F32)
        state[...] = jnp.zeros(state.shape, F32)

    xbuf[halo:halo + ts, :] = qkv_ref[...].astype(F32)

    ba = ba_ref[...]
    beta = _sigmoid(ba)
    g = -jnp.exp(alog_ref[...]) * _softplus(ba + dtb_ref[...])
    gc = jnp.dot(tri_ref[...], g, preferred_element_type=F32, precision=lax.Precision.HIGHEST)
    beta_t = beta.T
    gc_t = gc.T
    eg = jnp.exp(gc)
    eg_t = jnp.exp(gc_t)

    row = lax.broadcasted_iota(jnp.int32, (PAIR, PAIR), 0)
    col = lax.broadcasted_iota(jnp.int32, (PAIR, PAIR), 1)
    same = (row // CHUNK) == (col // CHUNK)
    causal = same & (row >= col)
    strict = same & (row > col)
    eye = jnp.where(row == col, 1.0, 0.0)
    n_levels = int(math.log2(CHUNK))
    level_mask = [(((row >> k) ^ (col >> k)) == 1) & (row > col) for k in range(n_levels)]

    def conv_silu(c0):
        acc = cw_ref[0:1, c0:c0 + LANES] * xbuf[halo - 3:halo - 3 + ts, c0:c0 + LANES]
        for k in range(1, A_CONV):
            acc = acc + cw_ref[k:k + 1, c0:c0 + LANES] * xbuf[halo - 3 + k:halo - 3 + k + ts, c0:c0 + LANES]
        return _silu(acc)

    def l2n(t):
        return t * lax.rsqrt(jnp.sum(t * t, axis=-1, keepdims=True) + L2_EPS)

    heads = range(A_HEADS)
    n_pairs = ts // PAIR
    q_b, k_b, v_b, k_t = [], [], [], []
    for h in heads:
        q_b.append((l2n(conv_silu(h * A_DK)) * (A_DK ** -0.5)).astype(BF16))
        kh = l2n(conv_silu(A_W + h * A_DK))
        k_b.append(kh.astype(BF16))
        k_t.append(kh.T)
        v_b.append(conv_silu(2 * A_W + h * A_DV).astype(BF16))

    units = [(h, d) for h in heads for d in range(n_pairs)]

    def rows(d):
        return slice(d * PAIR, (d + 1) * PAIR)

    def g_row_of(h, d):
        return gc_t[A_HEADS + h:A_HEADS + h + 1, rows(d)]

    decay = []
    for h, d in units:
        diff = gc[rows(d), A_HEADS + h:A_HEADS + h + 1] - g_row_of(h, d)
        decay.append(jnp.where(causal, jnp.exp(jnp.where(causal, diff, 0.0)), 0.0))
    kk = [_mm(k_b[h][rows(d)], k_t[h][:, rows(d)]) for h, d in units]
    qk = [(_mm(q_b[h][rows(d)], k_t[h][:, rows(d)]) * decay[j]).astype(BF16)
          for j, (h, d) in enumerate(units)]
    a_low = [jnp.where(strict, kk[j] * beta[rows(d), h:h + 1] * decay[j], 0.0)
             for j, (h, d) in enumerate(units)]
    t_inv = [eye - jnp.where(level_mask[0], a, 0.0) for a in a_low]
    for k in range(1, n_levels):
        x = [_mm(jnp.where(level_mask[k], a_low[j], 0.0), t_inv[j]) for j in range(len(units))]
        t_inv = [t_inv[j] - _mm(t_inv[j], x[j]) for j in range(len(units))]
    t_u = [t_inv[j] * beta_t[h:h + 1, rows(d)] for j, (h, d) in enumerate(units)]
    u = [_mm(t_u[j], v_b[h][rows(d)]) for j, (h, d) in enumerate(units)]
    w = [_mm(t_u[j] * eg_t[A_HEADS + h:A_HEADS + h + 1, rows(d)], k_b[h][rows(d)]).astype(BF16)
         for j, (h, d) in enumerate(units)]

    st = [state[h] for h in heads]
    lane_hi = lax.broadcasted_iota(jnp.int32, (1, PAIR), 1) >= CHUNK
    for d in range(n_pairs):
        v_new_lo = [None] * A_HEADS
        for half in range(2):
            r0 = d * PAIR + half * CHUNK
            sub = slice(half * CHUNK, (half + 1) * CHUNK)
            ws_qs = [_mm(jnp.concatenate([w[h * n_pairs + d][sub], q_b[h][r0:r0 + CHUNK]], axis=0), st[h])
                     for h in heads]
            v_new_b = [(u[h * n_pairs + d][sub] - ws_qs[h][:CHUNK]).astype(BF16) for h in heads]
            o_l = []
            for h in heads:
                j = h * n_pairs + d
                g_row = g_row_of(h, d)
                g_last = g_row[:, (half + 1) * CHUNK - 1:(half + 1) * CHUNK]
                e_row = jnp.exp(g_last - g_row)
                eg_col = eg[r0:r0 + CHUNK, A_HEADS + h:A_HEADS + h + 1]
                if half == 0:
                    rhs = v_new_b[h]
                    qk_c = qk[j][sub, 0:CHUNK]
                    kd_t = k_t[h][:, d * PAIR:d * PAIR + CHUNK] * e_row[:, 0:CHUNK]
                else:
                    rhs = jnp.concatenate([v_new_lo[h], v_new_b[h]], axis=0)
                    qk_c = qk[j][sub, :]
                    kd_t = k_t[h][:, rows(d)] * jnp.where(lane_hi, e_row, 0.0)
                o_l.append(eg_col * ws_qs[h][CHUNK:] + _mm(qk_c, rhs))
                st[h] = st[h] * jnp.exp(g_last) + _mm(kd_t, rhs)
            for h in heads:
                o = o_l[h]
                o = o * lax.rsqrt(jnp.mean(o * o, axis=-1, keepdims=True) + RMS_EPS)
                zh = z_ref[r0:r0 + CHUNK, h * A_DV:(h + 1) * A_DV].astype(F32)
                o_ref[r0:r0 + CHUNK, h * A_DV:(h + 1) * A_DV] = (o * nw_ref[...] * _silu(zh)).astype(o_ref.dtype)
            v_new_lo = v_new_b
    for h in heads:
        state[h] = st[h]

    xbuf[0:halo, :] = xbuf[ts:ts + halo, :]


def _deltanet(qkv, z, ba, conv_w, alog, dtb, norm_w, ts):
    b_, s_, _ = qkv.shape
    tri = np.tril(np.ones((CHUNK, CHUNK), np.float32))
    tri = jnp.asarray(np.kron(np.eye(ts // CHUNK, dtype=np.float32), tri))
    tok = lambda n: pl.BlockSpec((None, ts, n), lambda b, i: (b, i, 0))
    return pl.pallas_call(
        functools.partial(_deltanet_kernel, ts=ts),
        grid=(b_, s_ // ts),
        in_specs=[tok(3 * A_W), tok(A_W), tok(LANES), _const_spec(conv_w.shape),
                  _const_spec(alog.shape), _const_spec(dtb.shape), _const_spec(norm_w.shape),
                  _const_spec(tri.shape)],
        out_specs=tok(A_W),
        out_shape=jax.ShapeDtypeStruct((b_, s_, A_W), BF16),
        scratch_shapes=[pltpu.VMEM((ts + SUBLANES, 3 * A_W), F32),
                        pltpu.VMEM((A_HEADS, A_DK, A_DV), F32)],
        compiler_params=pltpu.CompilerParams(
            dimension_semantics=("parallel", "arbitrary"), vmem_limit_bytes=VMEM_LIMIT),
        name="deltanet",
    )(qkv, z, ba, conv_w, alog, dtb, norm_w, tri)


GROUP = 2 * CHUNK
GBAND = BAND + CHUNK


def _bandattn_kernel(q_ref, kp_ref, kc_ref, vp_ref, vc_ref, bias_ref, o_ref, kcat, vcat, *, tq):
    i = pl.program_id(1)
    kcat[0:tq, :] = kp_ref[...]
    kcat[tq:2 * tq, :] = kc_ref[...]
    vcat[0:tq, :] = vp_ref[...]
    vcat[tq:2 * tq, :] = vc_ref[...]
    scale = B_DH ** -0.5
    lane = lax.broadcasted_iota(jnp.int32, (GROUP, LANES), 1)
    low_half = lane < B_DH
    kpos = lax.broadcasted_iota(jnp.int32, (1, GBAND), 1)
    n_pairs = B_HEADS // 2

    def run(seq_start):
        def group_body(gi, carry):
            r0 = pl.multiple_of(gi * GROUP, GROUP)
            valid = (gi * GROUP - B_PREV_CHUNKS * CHUNK + kpos) >= 0

            def scores(p):
                c0 = p * LANES
                q_pair = q_ref[pl.ds(r0, GROUP), c0:c0 + LANES] * scale
                zero = jnp.zeros_like(q_pair)
                lhs = jnp.concatenate([jnp.where(low_half, q_pair, zero),
                                       jnp.where(low_half, zero, q_pair)], axis=0)
                return _mm_nt(lhs, kcat[pl.ds(r0, GBAND), c0:c0 + LANES])

            def finish(p, sc):
                c0 = p * LANES
                sc = sc + bias_ref[p]
                if seq_start:
                    sc = jnp.where(valid, sc, NEG_INF)
                m = jnp.max(sc, axis=-1, keepdims=True)
                e = jnp.exp(sc - m)
                den = jnp.sum(e, axis=-1, keepdims=True)
                o2 = _mm(e, vcat[pl.ds(r0, GBAND), c0:c0 + LANES]) * (1.0 / den)
                o_ref[pl.ds(r0, GROUP), c0:c0 + LANES] = jnp.where(
                    low_half, o2[:GROUP], o2[GROUP:]).astype(o_ref.dtype)

            sc_next = scores(0)
            for p in range(n_pairs):
                sc_cur = sc_next
                if p + 1 < n_pairs:
                    sc_next = scores(p + 1)
                finish(p, sc_cur)
            return carry

        lax.fori_loop(0, tq // GROUP, group_body, 0)

    @pl.when(i == 0)
    def _():
        run(True)

    @pl.when(i > 0)
    def _():
        run(False)


def _bandattn(qkv, bias, tq):
    b_, s_, _ = qkv.shape
    nb = B_W // B_W
    q_spec = pl.BlockSpec((None, tq, B_W), lambda b, i: (b, i, 0))
    kp_spec = pl.BlockSpec((None, tq, B_W), lambda b, i: (b, jnp.maximum(i - 1, 0), nb))
    kc_spec = pl.BlockSpec((None, tq, B_W), lambda b, i: (b, i, nb))
    vp_spec = pl.BlockSpec((None, tq, B_W), lambda b, i: (b, jnp.maximum(i - 1, 0), 2 * nb))
    vc_spec = pl.BlockSpec((None, tq, B_W), lambda b, i: (b, i, 2 * nb))
    return pl.pallas_call(
        functools.partial(_bandattn_kernel, tq=tq),
        grid=(b_, s_ // tq),
        in_specs=[q_spec, kp_spec, kc_spec, vp_spec, vc_spec, _const_spec(bias.shape)],
        out_specs=pl.BlockSpec((None, tq, B_W), lambda b, i: (b, i, 0)),
        out_shape=jax.ShapeDtypeStruct((b_, s_, B_W), BF16),
        scratch_shapes=[pltpu.VMEM((2 * tq, B_W), BF16), pltpu.VMEM((2 * tq, B_W), BF16)],
        compiler_params=pltpu.CompilerParams(
            dimension_semantics=("parallel", "arbitrary"), vmem_limit_bytes=VMEM_LIMIT),
        name="bandattn",
    )(qkv, qkv, qkv, qkv, qkv, bias)


def _merge_kernel(x_ref, oa_ref, ob_ref, g_ref, bg_ref, gt_ref, wa_ref, wb_ref, wo_ref,
                  lg_ref, lb_ref, o_ref, *, alpha):
    d = x_ref.shape[-1]
    gates = _sigmoid(g_ref[...].astype(F32) + bg_ref[...])
    merged = (gates[:, :d] * jnp.dot(oa_ref[...], wa_ref[...], preferred_element_type=F32)
              + gates[:, d:] * jnp.dot(ob_ref[...], wb_ref[...], preferred_element_type=F32))
    mix = jnp.dot(merged.astype(BF16), wo_ref[...], preferred_element_type=F32)
    y = alpha * x_ref[...] + gt_ref[...] * mix
    o_ref[...] = _layernorm(y, lg_ref[...], lb_ref[...])


def _merge(x, o_a, o_b, gates, b_gate, gate_t, wa, wb, wo, ln_g, ln_b, alpha, tm):
    b_, s_, d = x.shape
    tok = lambda n: pl.BlockSpec((None, tm, n), lambda b, i: (b, i, 0))
    vec = pl.BlockSpec((None, 1, d), lambda b, i: (b, 0, 0))
    return pl.pallas_call(
        functools.partial(_merge_kernel, alpha=alpha),
        grid=(b_, s_ // tm),
        in_specs=[tok(d), tok(A_W), tok(B_W), tok(2 * d), _const_spec(b_gate.shape), vec,
                  _const_spec(wa.shape), _const_spec(wb.shape), _const_spec(wo.shape),
                  _const_spec(ln_g.shape), _const_spec(ln_b.shape)],
        out_specs=tok(d),
        out_shape=jax.ShapeDtypeStruct((b_, s_, d), F32),
        compiler_params=pltpu.CompilerParams(
            dimension_semantics=("parallel", "parallel"), vmem_limit_bytes=VMEM_LIMIT),
        name="merge",
    )(x, o_a, o_b, gates, b_gate, gate_t, wa, wb, wo, ln_g, ln_b)


def _ffn_kernel(x_ref, sc_ref, sh_ref, gf_ref, wu_ref, cw_ref, cb_ref, wd_ref, lg_ref, lb_ref,
                o_ref, ubuf, carry, *, alpha, ts, d_ff, tf):
    s = pl.program_id(1)
    halo = SUBLANES

    @pl.when(s == 0)
    def _():
        carry[...] = jnp.zeros(carry.shape, F32)

    x = x_ref[...]
    h = (x * (1.0 + sc_ref[...]) + sh_ref[...]).astype(BF16)

    def up_proj(c0):
        return jnp.dot(h, wu_ref[:, c0:c0 + tf], preferred_element_type=F32)

    def conv(up, c0, slot):
        ubuf[slot, 0:halo, :] = carry[:, c0:c0 + tf]
        ubuf[slot, halo:halo + ts, :] = up
        carry[:, c0:c0 + tf] = up[ts - halo:ts, :]
        acc = cb_ref[0:1, c0:c0 + tf] + cw_ref[0:1, c0:c0 + tf] * ubuf[slot, halo - 2:halo - 2 + ts, :]
        for k in range(1, FFN_CONV):
            acc = acc + cw_ref[k:k + 1, c0:c0 + tf] * ubuf[slot, halo - 2 + k:halo - 2 + k + ts, :]
        return acc

    n_tiles = d_ff // tf
    ups = (up_proj(0), up_proj(d_ff))
    ffn = None
    for j in range(n_tiles):
        cur = ups
        if j + 1 < n_tiles:
            ups = (up_proj((j + 1) * tf), up_proj(d_ff + (j + 1) * tf))
        u_gate = conv(cur[0], j * tf, 2 * (j % 2))
        u_val = conv(cur[1], d_ff + j * tf, 2 * (j % 2) + 1)
        act = (_silu(u_gate) * u_val).astype(BF16)
        part = jnp.dot(act, wd_ref[j * tf:(j + 1) * tf, :], preferred_element_type=F32)
        ffn = part if ffn is None else ffn + part
    y = alpha * x + gf_ref[...] * ffn
    o_ref[...] = _layernorm(y, lg_ref[...], lb_ref[...])


def _ffn(x, scale, shift, gate_f, wu, cw, cb, wd, ln_g, ln_b, alpha, ts, tf):
    b_, s_, d = x.shape
    d_ff = wd.shape[0]
    tok = pl.BlockSpec((None, ts, d), lambda b, i: (b, i, 0))
    vec = pl.BlockSpec((None, 1, d), lambda b, i: (b, 0, 0))
    return pl.pallas_call(
        functools.partial(_ffn_kernel, alpha=alpha, ts=ts, d_ff=d_ff, tf=tf),
        grid=(b_, s_ // ts),
        in_specs=[tok, vec, vec, vec, _const_spec(wu.shape), _const_spec(cw.shape),
                  _const_spec(cb.shape), _const_spec(wd.shape), _const_spec(ln_g.shape),
                  _const_spec(ln_b.shape)],
        out_specs=tok,
        out_shape=jax.ShapeDtypeStruct((b_, s_, d), F32),
        scratch_shapes=[pltpu.VMEM((4, ts + SUBLANES, tf), F32), pltpu.VMEM((SUBLANES, 2 * d_ff), F32)],
        compiler_params=pltpu.CompilerParams(
            dimension_semantics=("parallel", "arbitrary"), vmem_limit_bytes=VMEM_LIMIT),
        name="ffn",
    )(x, scale, shift, gate_f, wu, cw, cb, wd, ln_g, ln_b)


def _rel_bias_table(rel):
    n_rel = rel.shape[1]
    top = B_PREV_CHUNKS * CHUNK + 2 * (CHUNK - 1)
    ext = jnp.concatenate([rel, jnp.broadcast_to(rel[:, -1:], (rel.shape[0], top + 1 - n_rel))], axis=1)
    rev = ext[:, ::-1]
    tab = jnp.stack([rev[:, CHUNK - 1 - i:CHUNK - 1 - i + BAND] for i in range(CHUNK)], axis=1)
    off = jnp.full((rel.shape[0], CHUNK, CHUNK), NEG_INF, F32)
    both = jnp.concatenate([jnp.concatenate([tab, off], axis=2),
                            jnp.concatenate([off, tab], axis=2)], axis=1)
    return both.reshape(rel.shape[0] // 2, 2 * GROUP, GBAND)


def _pad_lanes(v, offset):
    return jnp.zeros((1, LANES), F32).at[0, offset:offset + v.shape[0]].set(v)


def kernel(x, c, w_ada, b_ada, w_in, b_gate, conv_a, a_log, dt_bias, norm_a, rel_bias,
           w_branch_a, w_branch_b, w_o, ln1_g, ln1_b, w_up, conv_ffn, b_conv_ffn, w_down,
           ln2_g, ln2_b):
    b_, s_, d = x.shape
    depth = w_ada.shape[0]
    alpha = (2.0 * depth) ** 0.25
    c_pad = jnp.zeros((SUBLANES, d), F32).at[:b_].set(c)

    o0, o1, o2, o3, o4, o5, o6 = (3 * A_W, 4 * A_W, 4 * A_W + A_HEADS, 4 * A_W + 2 * A_HEADS,
                                  4 * A_W + 2 * A_HEADS + B_W, 4 * A_W + 2 * A_HEADS + 2 * B_W,
                                  4 * A_W + 2 * A_HEADS + 3 * B_W)
    for l in range(depth):
        mod = _ada(c_pad, w_ada[l], b_ada[l][None, :])[:b_]
        shift_t, scale_t, gate_t, shift_f, scale_f, gate_f = [
            m[:, None, :] for m in jnp.split(mod, 6, axis=-1)]

        wi = w_in[l]
        w_qkva = wi[:, :o0].astype(BF16)
        w_z = wi[:, o0:o1].astype(BF16)
        w_ba = jnp.zeros((d, LANES), BF16).at[:, :2 * A_HEADS].set(wi[:, o1:o3].astype(BF16))
        w_qkvb = wi[:, o3:o6].astype(BF16)
        w_g = wi[:, o6:].astype(BF16)
        qkva, z_a, ba, qkvb, gates = _inproj(x, scale_t, shift_t, w_qkva, w_z, w_ba, w_qkvb, w_g, tm=512)

        o_a = _deltanet(qkva, z_a, ba, conv_a[l], _pad_lanes(a_log[l], A_HEADS),
                        _pad_lanes(dt_bias[l], A_HEADS), norm_a[l][None, :], ts=256)
        bias = _rel_bias_table(rel_bias[l])
        o_b = _bandattn(qkvb, bias, tq=512)

        x = _merge(x, o_a, o_b, gates, b_gate[l][None, :], gate_t, w_branch_a[l].astype(BF16),
                   w_branch_b[l].astype(BF16), w_o[l].astype(BF16), ln1_g[l][None, :],
                   ln1_b[l][None, :], alpha, tm=512)
        x = _ffn(x, scale_f, shift_f, gate_f, w_up[l].astype(BF16), conv_ffn[l],
                 b_conv_ffn[l][None, :], w_down[l].astype(BF16), ln2_g[l][None, :],
                 ln2_b[l][None, :], alpha, ts=512, tf=256)
    return x
```

```python
import functools
import math

import jax
import jax.numpy as jnp
import numpy as np
from jax import lax
from jax.experimental import pallas as pl
from jax.experimental.pallas import tpu as pltpu

F32 = jnp.float32
BF16 = jnp.bfloat16

CHUNK = 64
A_HEADS = 8
A_DK = 128
A_DV = 128
A_CONV = 4
A_W = A_HEADS * A_DV
B_HEADS = 16
B_DH = 64
B_W = B_HEADS * B_DH
B_PREV_CHUNKS = 8
B_MAX_REL = 256
BAND = (B_PREV_CHUNKS + 1) * CHUNK
FFN_CONV = 3
LN_EPS = 1e-5
RMS_EPS = 1e-6
L2_EPS = 1e-6
NEG_INF = -1e30

LANES = 128
SUBLANES = 8
VMEM_LIMIT = 56 * 1024 * 1024


def _mm(a, b):
    return jnp.dot(a.astype(BF16), b.astype(BF16), preferred_element_type=F32)


def _mm_nt(a, b):
    return lax.dot_general(a.astype(BF16), b.astype(BF16), (((1,), (1,)), ((), ())),
                           preferred_element_type=F32)


def _sigmoid(x):
    return 1.0 / (1.0 + jnp.exp(-x))


def _silu(x):
    return x * _sigmoid(x)


def _softplus(x):
    return jnp.maximum(x, 0.0) + jnp.log(1.0 + jnp.exp(-jnp.abs(x)))


def _layernorm(y, g, b):
    mu = jnp.mean(y, axis=-1, keepdims=True)
    d = y - mu
    var = jnp.mean(d * d, axis=-1, keepdims=True)
    return d * lax.rsqrt(var + LN_EPS) * g + b


def _shift_rows(a, first_row):
    r = pltpu.roll(a, 1, axis=0)
    sub = lax.broadcasted_iota(jnp.int32, (SUBLANES, a.shape[1]), 0)
    head = jnp.where(sub == 0, first_row, r[0:SUBLANES])
    return jnp.concatenate([head, r[SUBLANES:]], axis=0)


def _causal_conv(x, prev, taps):
    acc = taps[0] * x
    for k in range(1, len(taps)):
        edge = taps[0] * prev[SUBLANES - k:SUBLANES - k + 1]
        for i in range(1, k):
            edge = edge + taps[i] * prev[SUBLANES - (k - i):SUBLANES - (k - i) + 1]
        acc = taps[k] * x + _shift_rows(acc, edge)
    return acc


def _const_spec(shape):
    return pl.BlockSpec(shape, lambda *_: (0,) * len(shape), pipeline_mode=pl.Buffered(1))


def _ada_kernel(c_ref, w_ref, b_ref, o_ref):
    c = c_ref[...]
    o_ref[...] = jnp.dot(_silu(c), w_ref[...], preferred_element_type=F32,
                         precision=lax.Precision.HIGHEST) + b_ref[...]


def _ada(c_pad, w, b):
    rows, d = c_pad.shape
    n = w.shape[1]
    tn = 1536
    return pl.pallas_call(
        _ada_kernel,
        grid=(n // tn,),
        in_specs=[pl.BlockSpec((rows, d), lambda j: (0, 0)),
                  pl.BlockSpec((d, tn), lambda j: (0, j)),
                  pl.BlockSpec((1, tn), lambda j: (0, j))],
        out_specs=pl.BlockSpec((rows, tn), lambda j: (0, j)),
        out_shape=jax.ShapeDtypeStruct((rows, n), F32),
        compiler_params=pltpu.CompilerParams(vmem_limit_bytes=VMEM_LIMIT),
        name="ada",
    )(c_pad, w, b)


INPROJ_TN = 256


def _inproj_kernel(x_ref, sc_ref, sh_ref, wa_ref, wz_ref, wba_ref, wb_ref, wg_ref, cw_ref,
                   oa_ref, oz_ref, oba_ref, ob_ref, og_ref, rawbuf0, rawbuf1, carry, *, tm):
    s = pl.program_id(1)
    halo = SUBLANES

    @pl.when(s == 0)
    def _():
        carry[...] = jnp.zeros(carry.shape, F32)

    h = (x_ref[...] * (1.0 + sc_ref[...]) + sh_ref[...]).astype(BF16)
    tn = INPROJ_TN
    rawbuf = (rawbuf0, rawbuf1)

    def plain(w_ref, o_ref, c0):
        n = min(tn, w_ref.shape[1])
        o_ref[:, c0:c0 + n] = jnp.dot(h, w_ref[:, c0:c0 + n], preferred_element_type=F32).astype(o_ref.dtype)

    def conv_project(c0, slot):
        raw = jnp.dot(h, wa_ref[:, c0:c0 + tn], preferred_element_type=F32)
        rawbuf[slot][0:halo, :] = carry[:, c0:c0 + tn]
        rawbuf[slot][halo:halo + tm, :] = raw
        carry[:, c0:c0 + tn] = raw[tm - halo:tm, :]

    def conv_act(c0, slot):
        taps = [cw_ref[k:k + 1, c0:c0 + tn] for k in range(A_CONV)]
        act = _silu(_causal_conv(rawbuf[slot][halo:halo + tm, :], rawbuf[slot][0:halo, :], taps))
        for g0 in range(0, tn, A_DK):
            t = act[:, g0:g0 + A_DK]
            if c0 < 2 * A_W:
                inv = lax.rsqrt(jnp.sum(t * t, axis=-1, keepdims=True) + L2_EPS)
                t = t * (inv * (A_DK ** -0.5) if c0 < A_W else inv)
            oa_ref[:, c0 + g0:c0 + g0 + A_DK] = t.astype(oa_ref.dtype)

    plain_jobs = [(w_ref, o_ref, j * tn) for w_ref, o_ref in ((wz_ref, oz_ref), (wb_ref, ob_ref), (wg_ref, og_ref))
                  for j in range(w_ref.shape[1] // tn)] + [(wba_ref, oba_ref, 0)]
    n_conv = wa_ref.shape[1] // tn
    conv_project(0, 0)
    conv_project(tn, 1)
    for j in range(n_conv):
        conv_act(j * tn, j % 2)
        if j + 2 < n_conv:
            conv_project((j + 2) * tn, j % 2)
        for _ in range(1 + (j % 3 == 0)):
            if plain_jobs:
                plain(*plain_jobs.pop(0))
    for job in plain_jobs:
        plain(*job)


def _inproj(x, scale, shift, wa, wz, wba, wb, wg, conv_w, tm):
    b_, s_, d = x.shape
    tok = lambda n: pl.BlockSpec((None, tm, n), lambda b, i: (b, i, 0))
    vec = pl.BlockSpec((None, 1, d), lambda b, i: (b, 0, 0))
    outs = [(wa.shape[1], BF16), (wz.shape[1], BF16), (wba.shape[1], F32),
            (wb.shape[1], BF16), (wg.shape[1], BF16)]
    return pl.pallas_call(
        functools.partial(_inproj_kernel, tm=tm),
        grid=(b_, s_ // tm),
        in_specs=[tok(d), vec, vec] + [_const_spec(w.shape) for w in (wa, wz, wba, wb, wg, conv_w)],
        out_specs=[tok(n) for n, _ in outs],
        out_shape=[jax.ShapeDtypeStruct((b_, s_, n), dt) for n, dt in outs],
        scratch_shapes=[pltpu.VMEM((tm + SUBLANES, INPROJ_TN), F32),
                        pltpu.VMEM((tm + SUBLANES, INPROJ_TN), F32),
                        pltpu.VMEM((SUBLANES, wa.shape[1]), F32)],
        compiler_params=pltpu.CompilerParams(
            dimension_semantics=("parallel", "arbitrary"), vmem_limit_bytes=VMEM_LIMIT),
        name="inproj",
    )(x, scale, shift, wa, wz, wba, wb, wg, conv_w)


PAIR = 2 * CHUNK


def _deltanet_kernel(qkv_ref, z_ref, ba_ref, alog_ref, dtb_ref, nw_ref, tri_ref, o_ref,
                     state, *, ts):
    s = pl.program_id(1)

    @pl.when(s == 0)
    def _():
        state[...] = jnp.zeros(state.shape, F32)

    ba = ba_ref[...]
    beta = _sigmoid(ba)
    g = -jnp.exp(alog_ref[...]) * _softplus(ba + dtb_ref[...])
    gc = jnp.dot(tri_ref[...], g, preferred_element_type=F32, precision=lax.Precision.HIGHEST)
    beta_t = beta.T
    gc_t = gc.T
    eg = jnp.exp(gc)
    eg_t = jnp.exp(gc_t)

    row = lax.broadcasted_iota(jnp.int32, (PAIR, PAIR), 0)
    col = lax.broadcasted_iota(jnp.int32, (PAIR, PAIR), 1)
    same = (row // CHUNK) == (col // CHUNK)
    causal = same & (row >= col)
    strict = same & (row > col)
    eye = jnp.where(row == col, 1.0, 0.0)
    n_levels = int(math.log2(CHUNK))
    level_mask = [(((row >> k) ^ (col >> k)) == 1) & (row > col) for k in range(n_levels)]

    heads = range(A_HEADS)
    n_pairs = ts // PAIR
    q_b = [qkv_ref[:, h * A_DK:(h + 1) * A_DK] for h in heads]
    k_b = [qkv_ref[:, A_W + h * A_DK:A_W + (h + 1) * A_DK] for h in heads]
    v_b = [qkv_ref[:, 2 * A_W + h * A_DV:2 * A_W + (h + 1) * A_DV] for h in heads]
    k_t = [k.astype(F32).T for k in k_b]

    units = [(h, d) for h in heads for d in range(n_pairs)]

    def rows(d):
        return slice(d * PAIR, (d + 1) * PAIR)

    def g_row_of(h, d):
        return gc_t[A_HEADS + h:A_HEADS + h + 1, rows(d)]

    decay = []
    for h, d in units:
        diff = gc[rows(d), A_HEADS + h:A_HEADS + h + 1] - g_row_of(h, d)
        decay.append(jnp.where(causal, jnp.exp(jnp.where(causal, diff, 0.0)), 0.0))
    kk = [_mm(k_b[h][rows(d)], k_t[h][:, rows(d)]) for h, d in units]
    qk = [(_mm(q_b[h][rows(d)], k_t[h][:, rows(d)]) * decay[j]).astype(BF16)
          for j, (h, d) in enumerate(units)]
    a_low = [jnp.where(strict, kk[j] * beta[rows(d), h:h + 1] * decay[j], 0.0)
             for j, (h, d) in enumerate(units)]
    t_inv = [eye - jnp.where(level_mask[0], a, 0.0) for a in a_low]
    for k in range(1, n_levels):
        x = [_mm(jnp.where(level_mask[k], a_low[j], 0.0), t_inv[j]) for j in range(len(units))]
        t_inv = [t_inv[j] - _mm(t_inv[j], x[j]) for j in range(len(units))]
    t_u = [t_inv[j] * beta_t[h:h + 1, rows(d)] for j, (h, d) in enumerate(units)]
    u = [_mm(t_u[j], v_b[h][rows(d)]) for j, (h, d) in enumerate(units)]
    w = [_mm(t_u[j] * eg_t[A_HEADS + h:A_HEADS + h + 1, rows(d)], k_b[h][rows(d)]).astype(BF16)
         for j, (h, d) in enumerate(units)]

    st = [state[h] for h in heads]
    lane_hi = lax.broadcasted_iota(jnp.int32, (1, PAIR), 1) >= CHUNK
    for d in range(n_pairs):
        v_new_lo = [None] * A_HEADS
        for half in range(2):
            r0 = d * PAIR + half * CHUNK
            sub = slice(half * CHUNK, (half + 1) * CHUNK)
            ws_qs = [_mm(jnp.concatenate([w[h * n_pairs + d][sub], q_b[h][r0:r0 + CHUNK]], axis=0), st[h])
                     for h in heads]
            v_new_b = [(u[h * n_pairs + d][sub] - ws_qs[h][:CHUNK]).astype(BF16) for h in heads]
            o_l = []
            for h in heads:
                j = h * n_pairs + d
                g_row = g_row_of(h, d)
                g_last = g_row[:, (half + 1) * CHUNK - 1:(half + 1) * CHUNK]
                e_row = jnp.exp(g_last - g_row)
                eg_col = eg[r0:r0 + CHUNK, A_HEADS + h:A_HEADS + h + 1]
                if half == 0:
                    rhs = v_new_b[h]
                    qk_c = qk[j][sub, 0:CHUNK]
                    kd_t = k_t[h][:, d * PAIR:d * PAIR + CHUNK] * e_row[:, 0:CHUNK]
                else:
                    rhs = jnp.concatenate([v_new_lo[h], v_new_b[h]], axis=0)
                    qk_c = qk[j][sub, :]
                    kd_t = k_t[h][:, rows(d)] * jnp.where(lane_hi, e_row, 0.0)
                o_l.append(eg_col * ws_qs[h][CHUNK:] + _mm(qk_c, rhs))
                st[h] = st[h] * jnp.exp(g_last) + _mm(kd_t, rhs)
            for h in heads:
                o = o_l[h]
                o = o * lax.rsqrt(jnp.mean(o * o, axis=-1, keepdims=True) + RMS_EPS)
                zh = z_ref[r0:r0 + CHUNK, h * A_DV:(h + 1) * A_DV].astype(F32)
                o_ref[r0:r0 + CHUNK, h * A_DV:(h + 1) * A_DV] = (o * nw_ref[...] * _silu(zh)).astype(o_ref.dtype)
            v_new_lo = v_new_b
    for h in heads:
        state[h] = st[h]


def _deltanet(qkv, z, ba, alog, dtb, norm_w, ts):
    b_, s_, _ = qkv.shape
    tri = np.tril(np.ones((CHUNK, CHUNK), np.float32))
    tri = jnp.asarray(np.kron(np.eye(ts // CHUNK, dtype=np.float32), tri))
    tok = lambda n: pl.BlockSpec((None, ts, n), lambda b, i: (b, i, 0))
    return pl.pallas_call(
        functools.partial(_deltanet_kernel, ts=ts),
        grid=(b_, s_ // ts),
        in_specs=[tok(3 * A_W), tok(A_W), tok(LANES),
                  _const_spec(alog.shape), _const_spec(dtb.shape), _const_spec(norm_w.shape),
                  _const_spec(tri.shape)],
        out_specs=tok(A_W),
        out_shape=jax.ShapeDtypeStruct((b_, s_, A_W), BF16),
        scratch_shapes=[pltpu.VMEM((A_HEADS, A_DK, A_DV), F32)],
        compiler_params=pltpu.CompilerParams(
            dimension_semantics=("parallel", "arbitrary"), vmem_limit_bytes=VMEM_LIMIT),
        name="deltanet",
    )(qkv, z, ba, alog, dtb, norm_w, tri)


GROUP = 2 * CHUNK
GBAND = BAND + CHUNK


def _bandattn_kernel(q_ref, kp_ref, kc_ref, vp_ref, vc_ref, bias_ref, o_ref, kcat, vcat, *, tq):
    i = pl.program_id(1)
    kcat[0:tq, :] = kp_ref[...]
    kcat[tq:2 * tq, :] = kc_ref[...]
    vcat[0:tq, :] = vp_ref[...]
    vcat[tq:2 * tq, :] = vc_ref[...]
    scale = B_DH ** -0.5
    lane = lax.broadcasted_iota(jnp.int32, (GROUP, LANES), 1)
    low_half = lane < B_DH
    kpos = lax.broadcasted_iota(jnp.int32, (1, GBAND), 1)
    n_pairs = B_HEADS // 2

    units = [(gi, p) for gi in range(tq // GROUP) for p in range(n_pairs)]

    def run(seq_start):
        def scores(gi, p):
            r0, c0 = gi * GROUP, p * LANES
            q_pair = q_ref[r0:r0 + GROUP, c0:c0 + LANES] * scale
            zero = jnp.zeros_like(q_pair)
            lhs = jnp.concatenate([jnp.where(low_half, q_pair, zero),
                                   jnp.where(low_half, zero, q_pair)], axis=0)
            return _mm_nt(lhs, kcat[r0:r0 + GBAND, c0:c0 + LANES])

        def row_max(gi, p, sc):
            sc = sc + bias_ref[p]
            if seq_start:
                sc = jnp.where((gi * GROUP - B_PREV_CHUNKS * CHUNK + kpos) >= 0, sc, NEG_INF)
            return sc, jnp.max(sc, axis=-1, keepdims=True)

        def finish(gi, p, sc, m):
            r0, c0 = gi * GROUP, p * LANES
            e = jnp.exp(sc - m)
            den = jnp.sum(e, axis=-1, keepdims=True)
            o2 = _mm(e, vcat[r0:r0 + GBAND, c0:c0 + LANES]) * (1.0 / den)
            o_ref[r0:r0 + GROUP, c0:c0 + LANES] = jnp.where(
                low_half, o2[:GROUP], o2[GROUP:]).astype(o_ref.dtype)

        n = len(units)
        sc_q, mx_q = {}, {}
        for t in range(n + 2):
            if t < n:
                sc_q[t] = scores(*units[t])
            if 0 <= t - 1 < n:
                mx_q[t - 1] = row_max(*units[t - 1], sc_q.pop(t - 1))
            if 0 <= t - 2 < n:
                finish(*units[t - 2], *mx_q.pop(t - 2))

    @pl.when(i == 0)
    def _():
        run(True)

    @pl.when(i > 0)
    def _():
        run(False)


def _bandattn(qkv, bias, tq):
    b_, s_, _ = qkv.shape
    nb = B_W // B_W
    q_spec = pl.BlockSpec((None, tq, B_W), lambda b, i: (b, i, 0))
    kp_spec = pl.BlockSpec((None, tq, B_W), lambda b, i: (b, jnp.maximum(i - 1, 0), nb))
    kc_spec = pl.BlockSpec((None, tq, B_W), lambda b, i: (b, i, nb))
    vp_spec = pl.BlockSpec((None, tq, B_W), lambda b, i: (b, jnp.maximum(i - 1, 0), 2 * nb))
    vc_spec = pl.BlockSpec((None, tq, B_W), lambda b, i: (b, i, 2 * nb))
    return pl.pallas_call(
        functools.partial(_bandattn_kernel, tq=tq),
        grid=(b_, s_ // tq),
        in_specs=[q_spec, kp_spec, kc_spec, vp_spec, vc_spec, _const_spec(bias.shape)],
        out_specs=pl.BlockSpec((None, tq, B_W), lambda b, i: (b, i, 0)),
        out_shape=jax.ShapeDtypeStruct((b_, s_, B_W), BF16),
        scratch_shapes=[pltpu.VMEM((2 * tq, B_W), BF16), pltpu.VMEM((2 * tq, B_W), BF16)],
        compiler_params=pltpu.CompilerParams(
            dimension_semantics=("parallel", "arbitrary"), vmem_limit_bytes=VMEM_LIMIT),
        name="bandattn",
    )(qkv, qkv, qkv, qkv, qkv, bias)


def _merge_kernel(x_ref, oa_ref, ob_ref, g_ref, bg_ref, gt_ref, wa_ref, wb_ref, wo_ref,
                  lg_ref, lb_ref, o_ref, *, alpha):
    d = x_ref.shape[-1]
    gates = _sigmoid(g_ref[...].astype(F32) + bg_ref[...])
    merged = (gates[:, :d] * jnp.dot(oa_ref[...], wa_ref[...], preferred_element_type=F32)
              + gates[:, d:] * jnp.dot(ob_ref[...], wb_ref[...], preferred_element_type=F32))
    mix = jnp.dot(merged.astype(BF16), wo_ref[...], preferred_element_type=F32)
    y = alpha * x_ref[...] + gt_ref[...] * mix
    o_ref[...] = _layernorm(y, lg_ref[...], lb_ref[...])


def _merge(x, o_a, o_b, gates, b_gate, gate_t, wa, wb, wo, ln_g, ln_b, alpha, tm):
    b_, s_, d = x.shape
    tok = lambda n: pl.BlockSpec((None, tm, n), lambda b, i: (b, i, 0))
    vec = pl.BlockSpec((None, 1, d), lambda b, i: (b, 0, 0))
    return pl.pallas_call(
        functools.partial(_merge_kernel, alpha=alpha),
        grid=(b_, s_ // tm),
        in_specs=[tok(d), tok(A_W), tok(B_W), tok(2 * d), _const_spec(b_gate.shape), vec,
                  _const_spec(wa.shape), _const_spec(wb.shape), _const_spec(wo.shape),
                  _const_spec(ln_g.shape), _const_spec(ln_b.shape)],
        out_specs=tok(d),
        out_shape=jax.ShapeDtypeStruct((b_, s_, d), F32),
        compiler_params=pltpu.CompilerParams(
            dimension_semantics=("parallel", "parallel"), vmem_limit_bytes=VMEM_LIMIT),
        name="merge",
    )(x, o_a, o_b, gates, b_gate, gate_t, wa, wb, wo, ln_g, ln_b)


def _ffn_kernel(x_ref, sc_ref, sh_ref, gf_ref, wu_ref, cw_ref, cb_ref, wd_ref, lg_ref, lb_ref,
                o_ref, ubuf0, ubuf1, ubuf2, ubuf3, carry, *, alpha, ts, d_ff, tf):
    s = pl.program_id(1)
    halo = SUBLANES

    @pl.when(s == 0)
    def _():
        carry[...] = jnp.zeros(carry.shape, F32)

    x = x_ref[...]
    h = (x * (1.0 + sc_ref[...]) + sh_ref[...]).astype(BF16)

    ubuf = (ubuf0, ubuf1, ubuf2, ubuf3)

    def up_proj(c0, slot):
        up = jnp.dot(h, wu_ref[:, c0:c0 + tf], preferred_element_type=F32)
        ubuf[slot][0:halo, :] = carry[:, c0:c0 + tf]
        ubuf[slot][halo:halo + ts, :] = up
        carry[:, c0:c0 + tf] = up[ts - halo:ts, :]

    def conv(c0, slot):
        taps = [cw_ref[k:k + 1, c0:c0 + tf] for k in range(FFN_CONV)]
        return _causal_conv(ubuf[slot][halo:halo + ts, :], ubuf[slot][0:halo, :], taps) + cb_ref[0:1, c0:c0 + tf]

    n_tiles = d_ff // tf
    up_proj(0, 0)
    up_proj(d_ff, 1)
    ffn = None
    for j in range(n_tiles):
        u_gate = conv(j * tf, 2 * (j % 2))
        u_val = conv(d_ff + j * tf, 2 * (j % 2) + 1)
        act = (_silu(u_gate) * u_val).astype(BF16)
        if j + 1 < n_tiles:
            up_proj((j + 1) * tf, 2 * ((j + 1) % 2))
            up_proj(d_ff + (j + 1) * tf, 2 * ((j + 1) % 2) + 1)
        part = jnp.dot(act, wd_ref[j * tf:(j + 1) * tf, :], preferred_element_type=F32)
        ffn = part if ffn is None else ffn + part
    y = alpha * x + gf_ref[...] * ffn
    o_ref[...] = _layernorm(y, lg_ref[...], lb_ref[...])


def _ffn(x, scale, shift, gate_f, wu, cw, cb, wd, ln_g, ln_b, alpha, ts, tf):
    b_, s_, d = x.shape
    d_ff = wd.shape[0]
    tok = pl.BlockSpec((None, ts, d), lambda b, i: (b, i, 0))
    vec = pl.BlockSpec((None, 1, d), lambda b, i: (b, 0, 0))
    return pl.pallas_call(
        functools.partial(_ffn_kernel, alpha=alpha, ts=ts, d_ff=d_ff, tf=tf),
        grid=(b_, s_ // ts),
        in_specs=[tok, vec, vec, vec, _const_spec(wu.shape), _const_spec(cw.shape),
                  _const_spec(cb.shape), _const_spec(wd.shape), _const_spec(ln_g.shape),
                  _const_spec(ln_b.shape)],
        out_specs=tok,
        out_shape=jax.ShapeDtypeStruct((b_, s_, d), F32),
        scratch_shapes=[pltpu.VMEM((ts + SUBLANES, tf), F32)] * 4 + [pltpu.VMEM((SUBLANES, 2 * d_ff), F32)],
        compiler_params=pltpu.CompilerParams(
            dimension_semantics=("parallel", "arbitrary"), vmem_limit_bytes=VMEM_LIMIT),
        name="ffn",
    )(x, scale, shift, gate_f, wu, cw, cb, wd, ln_g, ln_b)


def _rel_bias_table(rel):
    n_rel = rel.shape[1]
    top = B_PREV_CHUNKS * CHUNK + 2 * (CHUNK - 1)
    ext = jnp.concatenate([rel, jnp.broadcast_to(rel[:, -1:], (rel.shape[0], top + 1 - n_rel))], axis=1)
    rev = ext[:, ::-1]
    tab = jnp.stack([rev[:, CHUNK - 1 - i:CHUNK - 1 - i + BAND] for i in range(CHUNK)], axis=1)
    off = jnp.full((rel.shape[0], CHUNK, CHUNK), NEG_INF, F32)
    both = jnp.concatenate([jnp.concatenate([tab, off], axis=2),
                            jnp.concatenate([off, tab], axis=2)], axis=1)
    return both.reshape(rel.shape[0] // 2, 2 * GROUP, GBAND)


def _pad_lanes(v, offset):
    return jnp.zeros((1, LANES), F32).at[0, offset:offset + v.shape[0]].set(v)


def kernel(x, c, w_ada, b_ada, w_in, b_gate, conv_a, a_log, dt_bias, norm_a, rel_bias,
           w_branch_a, w_branch_b, w_o, ln1_g, ln1_b, w_up, conv_ffn, b_conv_ffn, w_down,
           ln2_g, ln2_b):
    b_, s_, d = x.shape
    depth = w_ada.shape[0]
    alpha = (2.0 * depth) ** 0.25
    c_pad = jnp.zeros((SUBLANES, d), F32).at[:b_].set(c)

    o0, o1, o2, o3, o4, o5, o6 = (3 * A_W, 4 * A_W, 4 * A_W + A_HEADS, 4 * A_W + 2 * A_HEADS,
                                  4 * A_W + 2 * A_HEADS + B_W, 4 * A_W + 2 * A_HEADS + 2 * B_W,
                                  4 * A_W + 2 * A_HEADS + 3 * B_W)
    for l in range(depth):
        mod = _ada(c_pad, w_ada[l], b_ada[l][None, :])[:b_]
        shift_t, scale_t, gate_t, shift_f, scale_f, gate_f = [
            m[:, None, :] for m in jnp.split(mod, 6, axis=-1)]

        wi = w_in[l]
        w_qkva = wi[:, :o0].astype(BF16)
        w_z = wi[:, o0:o1].astype(BF16)
        w_ba = jnp.zeros((d, LANES), BF16).at[:, :2 * A_HEADS].set(wi[:, o1:o3].astype(BF16))
        w_qkvb = wi[:, o3:o6].astype(BF16)
        w_g = wi[:, o6:].astype(BF16)
        qkva, z_a, ba, qkvb, gates = _inproj(x, scale_t, shift_t, w_qkva, w_z, w_ba, w_qkvb, w_g,
                                             conv_a[l], tm=512)

        o_a = _deltanet(qkva, z_a, ba, _pad_lanes(a_log[l], A_HEADS),
                        _pad_lanes(dt_bias[l], A_HEADS), norm_a[l][None, :], ts=256)
        bias = _rel_bias_table(rel_bias[l])
        o_b = _bandattn(qkvb, bias, tq=512)

        x = _merge(x, o_a, o_b, gates, b_gate[l][None, :], gate_t, w_branch_a[l].astype(BF16),
                   w_branch_b[l].astype(BF16), w_o[l].astype(BF16), ln1_g[l][None, :],
                   ln1_b[l][None, :], alpha, tm=512)
        x = _ffn(x, scale_f, shift_f, gate_f, w_up[l].astype(BF16), conv_ffn[l],
                 b_conv_ffn[l][None, :], w_down[l].astype(BF16), ln2_g[l][None, :],
                 ln2_b[l][None, :], alpha, ts=512, tf=256)
    return x
```

```python
import functools
import math

import jax
import jax.numpy as jnp
import numpy as np
from jax import lax
from jax.experimental import pallas as pl
from jax.experimental.pallas import tpu as pltpu

F32 = jnp.float32
BF16 = jnp.bfloat16

CHUNK = 64
A_HEADS = 8
A_DK = 128
A_DV = 128
A_CONV = 4
A_W = A_HEADS * A_DV
B_HEADS = 16
B_DH = 64
B_W = B_HEADS * B_DH
B_PREV_CHUNKS = 8
B_MAX_REL = 256
BAND = (B_PREV_CHUNKS + 1) * CHUNK
FFN_CONV = 3
LN_EPS = 1e-5
RMS_EPS = 1e-6
L2_EPS = 1e-6
NEG_INF = -1e30

LOG2E = math.log2(math.e)
ATTN_Q_SCALE = B_DH ** -0.5 * LOG2E
LANES = 128
SUBLANES = 8
VMEM_LIMIT = 56 * 1024 * 1024


def _mm(a, b):
    return jnp.dot(a.astype(BF16), b.astype(BF16), preferred_element_type=F32)


def _mm_nt(a, b):
    return lax.dot_general(a.astype(BF16), b.astype(BF16), (((1,), (1,)), ((), ())),
                           preferred_element_type=F32)


def _sigmoid(x):
    return 1.0 / (1.0 + jnp.exp(-x))


def _silu(x):
    return x * _sigmoid(x)


def _softplus(x):
    return jnp.maximum(x, 0.0) + jnp.log(1.0 + jnp.exp(-jnp.abs(x)))


def _layernorm(y, g, b):
    mu = jnp.mean(y, axis=-1, keepdims=True)
    d = y - mu
    var = jnp.mean(d * d, axis=-1, keepdims=True)
    return d * lax.rsqrt(var + LN_EPS) * g + b


def _shift_rows(a, first_row):
    r = pltpu.roll(a, 1, axis=0)
    sub = lax.broadcasted_iota(jnp.int32, (SUBLANES, a.shape[1]), 0)
    head = jnp.where(sub == 0, first_row, r[0:SUBLANES])
    return jnp.concatenate([head, r[SUBLANES:]], axis=0)


def _interleave_rows(a):
    t, n = a.shape
    return jnp.swapaxes(a.reshape(SUBLANES, t // SUBLANES, n), 0, 1).reshape(t, n)


def _deinterleave_rows(a):
    t, n = a.shape
    return jnp.swapaxes(a.reshape(t // SUBLANES, SUBLANES, n), 0, 1).reshape(t, n)


def _shift_rows_interleaved(a, first_row):
    t = a.shape[0]
    r = pltpu.roll(a[t - SUBLANES:t], 1, axis=0)
    sub = lax.broadcasted_iota(jnp.int32, r.shape, 0)
    return jnp.concatenate([jnp.where(sub == 0, first_row, r), a[:t - SUBLANES]], axis=0)


def _causal_conv(x, prev, taps, shift=_shift_rows):
    acc = taps[0] * x
    for k in range(1, len(taps)):
        edge = taps[0] * prev[SUBLANES - k:SUBLANES - k + 1]
        for i in range(1, k):
            edge = edge + taps[i] * prev[SUBLANES - (k - i):SUBLANES - (k - i) + 1]
        acc = taps[k] * x + shift(acc, edge)
    return acc


def _const_spec(shape):
    return pl.BlockSpec(shape, lambda *_: (0,) * len(shape), pipeline_mode=pl.Buffered(1))


def _ada_kernel(c_ref, w_ref, b_ref, o_ref):
    c = c_ref[...]
    o_ref[...] = jnp.dot(_silu(c), w_ref[...], preferred_element_type=F32,
                         precision=lax.Precision.HIGHEST) + b_ref[...]


def _ada(c_pad, w, b):
    rows, d = c_pad.shape
    n = w.shape[1]
    tn = 1536
    return pl.pallas_call(
        _ada_kernel,
        grid=(n // tn,),
        in_specs=[pl.BlockSpec((rows, d), lambda j: (0, 0)),
                  pl.BlockSpec((d, tn), lambda j: (0, j)),
                  pl.BlockSpec((1, tn), lambda j: (0, j))],
        out_specs=pl.BlockSpec((rows, tn), lambda j: (0, j)),
        out_shape=jax.ShapeDtypeStruct((rows, n), F32),
        compiler_params=pltpu.CompilerParams(vmem_limit_bytes=VMEM_LIMIT),
        name="ada",
    )(c_pad, w, b)


INPROJ_TN = 256


def _inproj_kernel(x_ref, sc_ref, sh_ref, wa_ref, wz_ref, wba_ref, wb_ref, wg_ref, cw_ref,
                   oa_ref, oz_ref, oba_ref, ob_ref, og_ref, rawbuf0, rawbuf1, carry, *, tm):
    s = pl.program_id(1)
    halo = SUBLANES

    @pl.when(s == 0)
    def _():
        carry[...] = jnp.zeros(carry.shape, F32)

    h = (x_ref[...] * (1.0 + sc_ref[...]) + sh_ref[...]).astype(BF16)
    tn = INPROJ_TN
    rawbuf = (rawbuf0, rawbuf1)

    def plain(w_ref, o_ref, c0):
        n = min(tn, w_ref.shape[1])
        raw = jnp.dot(h, w_ref[:, c0:c0 + n], preferred_element_type=F32)
        if o_ref is ob_ref and c0 < B_W:
            raw = raw * ATTN_Q_SCALE
        o_ref[:, c0:c0 + n] = raw.astype(o_ref.dtype)

    def conv_project(c0, slot):
        raw = jnp.dot(h, wa_ref[:, c0:c0 + tn], preferred_element_type=F32)
        rawbuf[slot][0:halo, :] = carry[:, c0:c0 + tn]
        rawbuf[slot][halo:halo + tm, :] = raw
        carry[:, c0:c0 + tn] = raw[tm - halo:tm, :]

    def conv_act(c0, slot):
        taps = [cw_ref[k:k + 1, c0:c0 + tn] for k in range(A_CONV)]
        act = _silu(_causal_conv(rawbuf[slot][halo:halo + tm, :], rawbuf[slot][0:halo, :], taps))
        for g0 in range(0, tn, A_DK):
            t = act[:, g0:g0 + A_DK]
            if c0 < 2 * A_W:
                inv = lax.rsqrt(jnp.sum(t * t, axis=-1, keepdims=True) + L2_EPS)
                t = t * (inv * (A_DK ** -0.5) if c0 < A_W else inv)
            oa_ref[:, c0 + g0:c0 + g0 + A_DK] = t.astype(oa_ref.dtype)

    plain_jobs = [(w_ref, o_ref, j * tn) for w_ref, o_ref in ((wz_ref, oz_ref), (wb_ref, ob_ref), (wg_ref, og_ref))
                  for j in range(w_ref.shape[1] // tn)] + [(wba_ref, oba_ref, 0)]
    n_conv = wa_ref.shape[1] // tn
    conv_project(0, 0)
    conv_project(tn, 1)
    for j in range(n_conv):
        conv_act(j * tn, j % 2)
        if j + 2 < n_conv:
            conv_project((j + 2) * tn, j % 2)
        for _ in range(1 + (j % 3 == 0)):
            if plain_jobs:
                plain(*plain_jobs.pop(0))
    for job in plain_jobs:
        plain(*job)


def _inproj(x, scale, shift, wa, wz, wba, wb, wg, conv_w, tm):
    b_, s_, d = x.shape
    tok = lambda n: pl.BlockSpec((None, tm, n), lambda b, i: (b, i, 0))
    vec = pl.BlockSpec((None, 1, d), lambda b, i: (b, 0, 0))
    outs = [(wa.shape[1], BF16), (wz.shape[1], BF16), (wba.shape[1], F32),
            (wb.shape[1], BF16), (wg.shape[1], BF16)]
    return pl.pallas_call(
        functools.partial(_inproj_kernel, tm=tm),
        grid=(b_, s_ // tm),
        in_specs=[tok(d), vec, vec] + [_const_spec(w.shape) for w in (wa, wz, wba, wb, wg, conv_w)],
        out_specs=[tok(n) for n, _ in outs],
        out_shape=[jax.ShapeDtypeStruct((b_, s_, n), dt) for n, dt in outs],
        scratch_shapes=[pltpu.VMEM((tm + SUBLANES, INPROJ_TN), F32),
                        pltpu.VMEM((tm + SUBLANES, INPROJ_TN), F32),
                        pltpu.VMEM((SUBLANES, wa.shape[1]), F32)],
        compiler_params=pltpu.CompilerParams(
            dimension_semantics=("parallel", "arbitrary"), vmem_limit_bytes=VMEM_LIMIT),
        name="inproj",
    )(x, scale, shift, wa, wz, wba, wb, wg, conv_w)


PAIR = 2 * CHUNK


def _deltanet_kernel(qkv_ref, z_ref, ba_ref, alog_ref, dtb_ref, nw_ref, tri_ref, o_ref,
                     state, *, ts):
    s = pl.program_id(1)

    @pl.when(s == 0)
    def _():
        state[...] = jnp.zeros(state.shape, F32)

    ba = ba_ref[...]
    beta = _sigmoid(ba)
    g = -jnp.exp(alog_ref[...]) * _softplus(ba + dtb_ref[...])
    g_hi = g.astype(BF16)
    g_mid = (g - g_hi.astype(F32)).astype(BF16)
    g_lo = (g - g_hi.astype(F32) - g_mid.astype(F32)).astype(BF16)
    tri = tri_ref[...]
    gc = (jnp.dot(tri, g_hi, preferred_element_type=F32)
          + (jnp.dot(tri, g_mid, preferred_element_type=F32)
             + jnp.dot(tri, g_lo, preferred_element_type=F32)))
    beta_t = beta.T
    gc_t = gc.T
    eg = jnp.exp(gc)
    eg_t = jnp.exp(gc_t)

    row = lax.broadcasted_iota(jnp.int32, (PAIR, PAIR), 0)
    col = lax.broadcasted_iota(jnp.int32, (PAIR, PAIR), 1)
    same = (row // CHUNK) == (col // CHUNK)
    causal = same & (row >= col)
    strict = same & (row > col)
    eye = jnp.where(row == col, 1.0, 0.0)
    n_levels = int(math.log2(CHUNK))
    level_mask = [(((row >> k) ^ (col >> k)) == 1) & (row > col) for k in range(n_levels)]

    heads = range(A_HEADS)
    n_pairs = ts // PAIR
    q_b = [qkv_ref[:, h * A_DK:(h + 1) * A_DK] for h in heads]
    k_b = [qkv_ref[:, A_W + h * A_DK:A_W + (h + 1) * A_DK] for h in heads]
    v_b = [qkv_ref[:, 2 * A_W + h * A_DV:2 * A_W + (h + 1) * A_DV] for h in heads]
    k_t = [k.astype(F32).T for k in k_b]

    units = [(h, d) for h in heads for d in range(n_pairs)]

    def rows(d):
        return slice(d * PAIR, (d + 1) * PAIR)

    def g_row_of(h, d):
        return gc_t[A_HEADS + h:A_HEADS + h + 1, rows(d)]

    decay = []
    for h, d in units:
        diff = gc[rows(d), A_HEADS + h:A_HEADS + h + 1] - g_row_of(h, d)
        decay.append(jnp.where(causal, jnp.exp(jnp.where(causal, diff, 0.0)), 0.0))
    kk = [_mm(k_b[h][rows(d)], k_t[h][:, rows(d)]) for h, d in units]
    qk = [(_mm(q_b[h][rows(d)], k_t[h][:, rows(d)]) * decay[j]).astype(BF16)
          for j, (h, d) in enumerate(units)]
    a_low = [jnp.where(strict, kk[j] * beta[rows(d), h:h + 1] * decay[j], 0.0)
             for j, (h, d) in enumerate(units)]
    t_inv = [eye - jnp.where(level_mask[0], a, 0.0) for a in a_low]
    for k in range(1, n_levels):
        x = [_mm(jnp.where(level_mask[k], a_low[j], 0.0), t_inv[j]) for j in range(len(units))]
        t_inv = [t_inv[j] - _mm(t_inv[j], x[j]) for j in range(len(units))]
    t_u = [t_inv[j] * beta_t[h:h + 1, rows(d)] for j, (h, d) in enumerate(units)]
    u = [_mm(t_u[j], v_b[h][rows(d)]) for j, (h, d) in enumerate(units)]
    w = [_mm(t_u[j] * eg_t[A_HEADS + h:A_HEADS + h + 1, rows(d)], k_b[h][rows(d)]).astype(BF16)
         for j, (h, d) in enumerate(units)]

    st = [state[h] for h in heads]
    lane_hi = lax.broadcasted_iota(jnp.int32, (1, PAIR), 1) >= CHUNK
    for d in range(n_pairs):
        v_new_lo = [None] * A_HEADS
        for half in range(2):
            r0 = d * PAIR + half * CHUNK
            sub = slice(half * CHUNK, (half + 1) * CHUNK)
            ws_qs = [_mm(jnp.concatenate([w[h * n_pairs + d][sub], q_b[h][r0:r0 + CHUNK]], axis=0), st[h])
                     for h in heads]
            v_new_b = [(u[h * n_pairs + d][sub] - ws_qs[h][:CHUNK]).astype(BF16) for h in heads]
            o_l = []
            for h in heads:
                j = h * n_pairs + d
                g_row = g_row_of(h, d)
                g_last = g_row[:, (half + 1) * CHUNK - 1:(half + 1) * CHUNK]
                e_row = jnp.exp(g_last - g_row)
                eg_col = eg[r0:r0 + CHUNK, A_HEADS + h:A_HEADS + h + 1]
                if half == 0:
                    rhs = v_new_b[h]
                    qk_c = qk[j][sub, 0:CHUNK]
                    kd_t = k_t[h][:, d * PAIR:d * PAIR + CHUNK] * e_row[:, 0:CHUNK]
                else:
                    rhs = jnp.concatenate([v_new_lo[h], v_new_b[h]], axis=0)
                    qk_c = qk[j][sub, :]
                    kd_t = k_t[h][:, rows(d)] * jnp.where(lane_hi, e_row, 0.0)
                o_l.append(eg_col * ws_qs[h][CHUNK:] + _mm(qk_c, rhs))
                st[h] = st[h] * jnp.exp(g_last) + _mm(kd_t, rhs)
            for h in heads:
                o = o_l[h]
                o = o * lax.rsqrt(jnp.mean(o * o, axis=-1, keepdims=True) + RMS_EPS)
                zh = z_ref[r0:r0 + CHUNK, h * A_DV:(h + 1) * A_DV].astype(F32)
                o_ref[r0:r0 + CHUNK, h * A_DV:(h + 1) * A_DV] = (o * nw_ref[...] * _silu(zh)).astype(o_ref.dtype)
            v_new_lo = v_new_b
    for h in heads:
        state[h] = st[h]


def _deltanet(qkv, z, ba, alog, dtb, norm_w, ts):
    b_, s_, _ = qkv.shape
    tri = np.tril(np.ones((CHUNK, CHUNK), np.float32))
    tri = jnp.asarray(np.kron(np.eye(ts // CHUNK, dtype=np.float32), tri), dtype=BF16)
    tok = lambda n: pl.BlockSpec((None, ts, n), lambda b, i: (b, i, 0))
    return pl.pallas_call(
        functools.partial(_deltanet_kernel, ts=ts),
        grid=(b_, s_ // ts),
        in_specs=[tok(3 * A_W), tok(A_W), tok(LANES),
                  _const_spec(alog.shape), _const_spec(dtb.shape), _const_spec(norm_w.shape),
                  _const_spec(tri.shape)],
        out_specs=tok(A_W),
        out_shape=jax.ShapeDtypeStruct((b_, s_, A_W), BF16),
        scratch_shapes=[pltpu.VMEM((A_HEADS, A_DK, A_DV), F32)],
        compiler_params=pltpu.CompilerParams(
            dimension_semantics=("parallel", "arbitrary"), vmem_limit_bytes=VMEM_LIMIT),
        name="deltanet",
    )(qkv, z, ba, alog, dtb, norm_w, tri)


GROUP = 2 * CHUNK
GBAND = BAND + CHUNK


def _bandattn_kernel(q_ref, kp_ref, kc_ref, vp_ref, vc_ref, bias_ref, o_ref, kcat, vcat, *, tq):
    i = pl.program_id(1)
    kcat[0:tq, :] = kp_ref[...]
    kcat[tq:2 * tq, :] = kc_ref[...]
    vcat[0:tq, :] = vp_ref[...]
    vcat[tq:2 * tq, :] = vc_ref[...]
    lane = lax.broadcasted_iota(jnp.int32, (GROUP, LANES), 1)
    low_half = lane < B_DH
    kpos = lax.broadcasted_iota(jnp.int32, (1, GBAND), 1)
    n_pairs = B_HEADS // 2

    units = [(gi, p) for gi in range(tq // GROUP) for p in range(n_pairs)]

    def run(seq_start):
        def scores(gi, p):
            r0, c0 = gi * GROUP, p * LANES
            q_pair = q_ref[r0:r0 + GROUP, c0:c0 + LANES]
            zero = jnp.zeros_like(q_pair)
            lhs = jnp.concatenate([jnp.where(low_half, q_pair, zero),
                                   jnp.where(low_half, zero, q_pair)], axis=0)
            return _mm_nt(lhs, kcat[r0:r0 + GBAND, c0:c0 + LANES])

        def row_max(gi, p, sc):
            sc = sc + bias_ref[p]
            if seq_start:
                sc = jnp.where((gi * GROUP - B_PREV_CHUNKS * CHUNK + kpos) >= 0, sc, NEG_INF)
            return sc, jnp.max(sc, axis=-1, keepdims=True)

        def finish(gi, p, sc, m):
            r0, c0 = gi * GROUP, p * LANES
            e = jnp.exp2(sc - m)
            den = jnp.sum(e, axis=-1, keepdims=True)
            o2 = _mm(e, vcat[r0:r0 + GBAND, c0:c0 + LANES]) * (1.0 / den)
            o_ref[r0:r0 + GROUP, c0:c0 + LANES] = jnp.where(
                low_half, o2[:GROUP], o2[GROUP:]).astype(o_ref.dtype)

        n = len(units)
        sc_q, mx_q = {}, {}
        for t in range(n + 2):
            if t < n:
                sc_q[t] = scores(*units[t])
            if 0 <= t - 1 < n:
                mx_q[t - 1] = row_max(*units[t - 1], sc_q.pop(t - 1))
            if 0 <= t - 2 < n:
                finish(*units[t - 2], *mx_q.pop(t - 2))

    @pl.when(i == 0)
    def _():
        run(True)

    @pl.when(i > 0)
    def _():
        run(False)


def _bandattn(qkv, bias, tq):
    b_, s_, _ = qkv.shape
    nb = B_W // B_W
    q_spec = pl.BlockSpec((None, tq, B_W), lambda b, i: (b, i, 0))
    kp_spec = pl.BlockSpec((None, tq, B_W), lambda b, i: (b, jnp.maximum(i - 1, 0), nb))
    kc_spec = pl.BlockSpec((None, tq, B_W), lambda b, i: (b, i, nb))
    vp_spec = pl.BlockSpec((None, tq, B_W), lambda b, i: (b, jnp.maximum(i - 1, 0), 2 * nb))
    vc_spec = pl.BlockSpec((None, tq, B_W), lambda b, i: (b, i, 2 * nb))
    return pl.pallas_call(
        functools.partial(_bandattn_kernel, tq=tq),
        grid=(b_, s_ // tq),
        in_specs=[q_spec, kp_spec, kc_spec, vp_spec, vc_spec, _const_spec(bias.shape)],
        out_specs=pl.BlockSpec((None, tq, B_W), lambda b, i: (b, i, 0)),
        out_shape=jax.ShapeDtypeStruct((b_, s_, B_W), BF16),
        scratch_shapes=[pltpu.VMEM((2 * tq, B_W), BF16), pltpu.VMEM((2 * tq, B_W), BF16)],
        compiler_params=pltpu.CompilerParams(
            dimension_semantics=("parallel", "arbitrary"), vmem_limit_bytes=VMEM_LIMIT),
        name="bandattn",
    )(qkv, qkv, qkv, qkv, qkv, bias)


def _merge_kernel(x_ref, oa_ref, ob_ref, g_ref, bg_ref, gt_ref, wa_ref, wb_ref, wo_ref,
                  lg_ref, lb_ref, o_ref, *, alpha):
    d = x_ref.shape[-1]
    gates = _sigmoid(g_ref[...].astype(F32) + bg_ref[...])
    merged = (gates[:, :d] * jnp.dot(oa_ref[...], wa_ref[...], preferred_element_type=F32)
              + gates[:, d:] * jnp.dot(ob_ref[...], wb_ref[...], preferred_element_type=F32))
    mix = jnp.dot(merged.astype(BF16), wo_ref[...], preferred_element_type=F32)
    y = alpha * x_ref[...] + gt_ref[...] * mix
    o_ref[...] = _layernorm(y, lg_ref[...], lb_ref[...])


def _merge(x, o_a, o_b, gates, b_gate, gate_t, wa, wb, wo, ln_g, ln_b, alpha, tm):
    b_, s_, d = x.shape
    tok = lambda n: pl.BlockSpec((None, tm, n), lambda b, i: (b, i, 0))
    vec = pl.BlockSpec((None, 1, d), lambda b, i: (b, 0, 0))
    return pl.pallas_call(
        functools.partial(_merge_kernel, alpha=alpha),
        grid=(b_, s_ // tm),
        in_specs=[tok(d), tok(A_W), tok(B_W), tok(2 * d), _const_spec(b_gate.shape), vec,
                  _const_spec(wa.shape), _const_spec(wb.shape), _const_spec(wo.shape),
                  _const_spec(ln_g.shape), _const_spec(ln_b.shape)],
        out_specs=tok(d),
        out_shape=jax.ShapeDtypeStruct((b_, s_, d), F32),
        compiler_params=pltpu.CompilerParams(
            dimension_semantics=("parallel", "parallel"), vmem_limit_bytes=VMEM_LIMIT),
        name="merge",
    )(x, o_a, o_b, gates, b_gate, gate_t, wa, wb, wo, ln_g, ln_b)


def _ffn_kernel(x_ref, sc_ref, sh_ref, gf_ref, wu_ref, cw_ref, cb_ref, wd_ref, lg_ref, lb_ref,
                o_ref, ubuf0, ubuf1, ubuf2, ubuf3, carry, act_buf, *, alpha, ts, d_ff, tf):
    s = pl.program_id(1)
    halo = SUBLANES

    @pl.when(s == 0)
    def _():
        carry[...] = jnp.zeros(carry.shape, F32)

    x = _interleave_rows(x_ref[...])
    h = (x * (1.0 + sc_ref[...]) + sh_ref[...]).astype(BF16)
    ubuf = (ubuf0, ubuf1, ubuf2, ubuf3)

    def up_proj(c0, slot):
        up = jnp.dot(h, wu_ref[:, c0:c0 + tf], preferred_element_type=F32)
        ubuf[slot][0:halo, :] = carry[:, c0:c0 + tf]
        ubuf[slot][halo:halo + ts, :] = up
        for k in range(1, FFN_CONV):
            r = ts - 1 - (k - 1) * SUBLANES
            carry[halo - k:halo - k + 1, c0:c0 + tf] = up[r:r + 1, :]

    def conv(c0, slot):
        taps = [cw_ref[k:k + 1, c0:c0 + tf] for k in range(FFN_CONV)]
        return _causal_conv(ubuf[slot][halo:halo + ts, :], ubuf[slot][0:halo, :], taps,
                            _shift_rows_interleaved) + cb_ref[0:1, c0:c0 + tf]

    n_tiles = d_ff // tf
    up_proj(0, 0)
    up_proj(d_ff, 1)
    for j in range(n_tiles):
        if j + 1 < n_tiles:
            up_proj((j + 1) * tf, 2 * ((j + 1) % 2))
            up_proj(d_ff + (j + 1) * tf, 2 * ((j + 1) % 2) + 1)
        u_gate = conv(j * tf, 2 * (j % 2))
        u_val = conv(d_ff + j * tf, 2 * (j % 2) + 1)
        act_buf[:, j * tf:(j + 1) * tf] = (_silu(u_gate) * u_val).astype(BF16)
    ffn = jnp.dot(act_buf[...], wd_ref[...], preferred_element_type=F32)
    y = alpha * x + gf_ref[...] * ffn
    o_ref[...] = _deinterleave_rows(_layernorm(y, lg_ref[...], lb_ref[...]))


def _ffn(x, scale, shift, gate_f, wu, cw, cb, wd, ln_g, ln_b, alpha, ts, tf):
    b_, s_, d = x.shape
    d_ff = wd.shape[0]
    tok = pl.BlockSpec((None, ts, d), lambda b, i: (b, i, 0))
    vec = pl.BlockSpec((None, 1, d), lambda b, i: (b, 0, 0))
    return pl.pallas_call(
        functools.partial(_ffn_kernel, alpha=alpha, ts=ts, d_ff=d_ff, tf=tf),
        grid=(b_, s_ // ts),
        in_specs=[tok, vec, vec, vec, _const_spec(wu.shape), _const_spec(cw.shape),
                  _const_spec(cb.shape), _const_spec(wd.shape), _const_spec(ln_g.shape),
                  _const_spec(ln_b.shape)],
        out_specs=tok,
        out_shape=jax.ShapeDtypeStruct((b_, s_, d), F32),
        scratch_shapes=[pltpu.VMEM((ts + SUBLANES, tf), F32)] * 4
        + [pltpu.VMEM((SUBLANES, 2 * d_ff), F32), pltpu.VMEM((ts, d_ff), BF16)],
        compiler_params=pltpu.CompilerParams(
            dimension_semantics=("parallel", "arbitrary"), vmem_limit_bytes=VMEM_LIMIT),
        name="ffn",
    )(x, scale, shift, gate_f, wu, cw, cb, wd, ln_g, ln_b)


def _rel_bias_table(rel):
    n_rel = rel.shape[1]
    top = B_PREV_CHUNKS * CHUNK + 2 * (CHUNK - 1)
    ext = jnp.concatenate([rel, jnp.broadcast_to(rel[:, -1:], (rel.shape[0], top + 1 - n_rel))], axis=1)
    rev = ext[:, ::-1]
    tab = jnp.stack([rev[:, CHUNK - 1 - i:CHUNK - 1 - i + BAND] for i in range(CHUNK)], axis=1) * LOG2E
    off = jnp.full((rel.shape[0], CHUNK, CHUNK), NEG_INF, F32)
    both = jnp.concatenate([jnp.concatenate([tab, off], axis=2),
                            jnp.concatenate([off, tab], axis=2)], axis=1)
    return both.reshape(rel.shape[0] // 2, 2 * GROUP, GBAND)


def _pad_lanes(v, offset):
    return jnp.zeros((1, LANES), F32).at[0, offset:offset + v.shape[0]].set(v)


def kernel(x, c, w_ada, b_ada, w_in, b_gate, conv_a, a_log, dt_bias, norm_a, rel_bias,
           w_branch_a, w_branch_b, w_o, ln1_g, ln1_b, w_up, conv_ffn, b_conv_ffn, w_down,
           ln2_g, ln2_b):
    b_, s_, d = x.shape
    depth = w_ada.shape[0]
    alpha = (2.0 * depth) ** 0.25
    c_pad = jnp.zeros((SUBLANES, d), F32).at[:b_].set(c)

    o0, o1, o2, o3, o4, o5, o6 = (3 * A_W, 4 * A_W, 4 * A_W + A_HEADS, 4 * A_W + 2 * A_HEADS,
                                  4 * A_W + 2 * A_HEADS + B_W, 4 * A_W + 2 * A_HEADS + 2 * B_W,
                                  4 * A_W + 2 * A_HEADS + 3 * B_W)
    for l in range(depth):
        mod = _ada(c_pad, w_ada[l], b_ada[l][None, :])[:b_]
        shift_t, scale_t, gate_t, shift_f, scale_f, gate_f = [
            m[:, None, :] for m in jnp.split(mod, 6, axis=-1)]

        wi = w_in[l]
        w_qkva = wi[:, :o0].astype(BF16)
        w_z = wi[:, o0:o1].astype(BF16)
        w_ba = jnp.zeros((d, LANES), BF16).at[:, :2 * A_HEADS].set(wi[:, o1:o3].astype(BF16))
        w_qkvb = wi[:, o3:o6].astype(BF16)
        w_g = wi[:, o6:].astype(BF16)
        qkva, z_a, ba, qkvb, gates = _inproj(x, scale_t, shift_t, w_qkva, w_z, w_ba, w_qkvb, w_g,
                                             conv_a[l], tm=512)

        o_a = _deltanet(qkva, z_a, ba, _pad_lanes(a_log[l], A_HEADS),
                        _pad_lanes(dt_bias[l], A_HEADS), norm_a[l][None, :], ts=256)
        bias = _rel_bias_table(rel_bias[l])
        o_b = _bandattn(qkvb, bias, tq=512)

        x = _merge(x, o_a, o_b, gates, b_gate[l][None, :], gate_t, w_branch_a[l].astype(BF16),
                   w_branch_b[l].astype(BF16), w_o[l].astype(BF16), ln1_g[l][None, :],
                   ln1_b[l][None, :], alpha, tm=512)
        x = _ffn(x, scale_f, shift_f, gate_f, w_up[l].astype(BF16), conv_ffn[l],
                 b_conv_ffn[l][None, :], w_down[l].astype(BF16), ln2_g[l][None, :],
                 ln2_b[l][None, :], alpha, ts=512, tf=256)
    return x
```
